```python
import math
import jax, jax.numpy as jnp
from jax import lax
import numpy as np

D_MODEL = 1024
BATCH = 4
SEQ = 4096
DEPTH = 4
DEC_BATCH = 128
DEC_SEQ = 4
PAST_LEN = 2048
PAGE_SIZE = 128

HEAD_DIM = 64
ATTN_SCALE = HEAD_DIM ** -0.5
EPS = 1e-6
A_WIDTH = D_MODEL // 2
A_GROUPS = 4
A_GROUP_DIM = A_WIDTH // A_GROUPS
CHUNK = 128
B_HEADS = D_MODEL // 128
B_WIDTH = B_HEADS * HEAD_DIM
FOX_Q_BLOCK = 128
FORGET_BIAS_INIT = 3.0
C_HEADS = D_MODEL // 128
C_WIDTH = C_HEADS * HEAD_DIM
MOBA_BLOCK = 256
MOBA_TOPK = 3
MOBA_Q_BLOCK = 32
N_BUCKETS = 32
MAX_DISTANCE = 128
N_BRANCHES = 3
N_IN = 3 * A_WIDTH + 4 * B_WIDTH + B_HEADS + 4 * C_WIDTH + N_BRANCHES * D_MODEL

kernel_name = 'hybrid_gmlp_fox_moba_decoder'


def rmsnorm(x, g):
    xf = x.astype(jnp.float32)
    y = xf * lax.rsqrt(jnp.mean(xf * xf, axis=-1, keepdims=True) + EPS)
    return y.astype(x.dtype) * g


def split_cols(p):
    sizes = (A_WIDTH, A_WIDTH, A_WIDTH,
             B_WIDTH, B_WIDTH, B_WIDTH, B_HEADS, B_WIDTH,
             C_WIDTH, C_WIDTH, C_WIDTH, C_WIDTH,
             D_MODEL, D_MODEL, D_MODEL)
    out, off = [], 0
    for s in sizes:
        out.append(p[..., off:off + s])
        off += s
    return out


def t5_bucket(rel):
    n = jnp.maximum(rel, 0)
    max_exact = N_BUCKETS // 2
    nf = jnp.maximum(n, 1).astype(jnp.float32)
    large = max_exact + (jnp.log(nf / max_exact) / math.log(MAX_DISTANCE / max_exact) * (N_BUCKETS - max_exact)).astype(jnp.int32)
    return jnp.where(n < max_exact, n, jnp.minimum(large, N_BUCKETS - 1))


def sweep_queries(block_fn, q, q_pos0, block):
    bsz, tq = q.shape[:2]
    if tq <= block:
        return block_fn(q, q_pos0 + jnp.arange(tq))
    nb = -(-tq // block)
    qp = jnp.pad(q, ((0, 0), (0, nb * block - tq), (0, 0), (0, 0)))
    qb = jnp.swapaxes(qp.reshape((bsz, nb, block) + q.shape[2:]), 0, 1)
    pos = (q_pos0 + jnp.arange(nb * block)).reshape(nb, block)
    out = lax.map(lambda a: block_fn(a[0], a[1]), (qb, pos))
    out = jnp.swapaxes(out, 0, 1).reshape((bsz, nb * block) + out.shape[3:])
    return out[:, :tq]


def chunk_spatial_mix(v, w_s, b_s):
    bsz, t = v.shape[:2]
    n_chunks = -(-t // CHUNK)
    vp = jnp.pad(v, ((0, 0), (0, n_chunks * CHUNK - t), (0, 0), (0, 0)))
    vp = vp.reshape(bsz, n_chunks, CHUNK, A_GROUPS, A_GROUP_DIM)
    w = jnp.where(jnp.tril(jnp.ones((CHUNK, CHUNK), dtype=bool)), w_s, 0)
    s = jnp.einsum('gts,bnsgc->bntgc', w, vp) + b_s.T[None, None, :, :, None]
    return s.reshape(bsz, n_chunks * CHUNK, A_GROUPS, A_GROUP_DIM)[:, :t]


def fox_block(q, k, v, cum_f, q_pos):
    s = jnp.einsum('bqhd,bkhd->bhqk', q, k).astype(jnp.float32) * ATTN_SCALE
    f_q = jnp.take(cum_f, q_pos, axis=1, mode='clip')
    decay = jnp.swapaxes(f_q, 1, 2)[..., :, None] - jnp.swapaxes(cum_f, 1, 2)[..., None, :]
    causal = q_pos[:, None] >= jnp.arange(k.shape[1])[None, :]
    s = jnp.where(causal, s + decay, -jnp.inf)
    p = jax.nn.softmax(s, axis=-1).astype(v.dtype)
    return jnp.einsum('bhqk,bkhd->bqhd', p, v)


def fox_attention(q, k, v, logf, q_pos0):
    cum_f = jnp.cumsum(logf.astype(jnp.float32), axis=1)
    return sweep_queries(lambda qb, pos: fox_block(qb, k, v, cum_f, pos), q, q_pos0, FOX_Q_BLOCK)


def moba_block(q, kb, vb, k_means, rel_bias, q_pos):
    bsz, tq = q.shape[:2]
    n_blk = k_means.shape[2]
    topk = min(MOBA_TOPK, n_blk)
    qh = jnp.swapaxes(q, 1, 2)
    own = q_pos // MOBA_BLOCK
    gate = jnp.einsum('bhqd,bhnd->bhqn', qh, k_means).astype(jnp.float32)
    fully_past = jnp.arange(n_blk)[None, :] < own[:, None]
    gate = jnp.where(fully_past, gate, -jnp.inf)
    _, sel = lax.top_k(gate, topk)
    own_b = jnp.broadcast_to(own[None, None, :, None], (bsz, C_HEADS, tq, 1)).astype(sel.dtype)
    blocks = jnp.concatenate([sel, own_b], axis=-1)
    valid = jnp.concatenate([sel < own[None, None, :, None], jnp.ones_like(own_b, dtype=bool)], axis=-1)
    gather = jax.vmap(jax.vmap(lambda table, idx: table[idx]))
    kg = gather(kb, blocks)
    vg = gather(vb, blocks)
    s = jnp.einsum('bhqd,bhqnkd->bhqnk', qh, kg).astype(jnp.float32) * ATTN_SCALE
    k_pos = blocks[..., None] * MOBA_BLOCK + jnp.arange(MOBA_BLOCK)
    rel = q_pos[None, None, :, None, None] - k_pos
    bias = rel_bias.T[jnp.arange(C_HEADS)[None, :, None, None, None], t5_bucket(rel)]
    s = jnp.where(valid[..., None] & (rel >= 0), s + bias, -jnp.inf)
    p = jax.nn.softmax(s.reshape(bsz, C_HEADS, tq, -1), axis=-1).reshape(s.shape).astype(vg.dtype)
    return jnp.einsum('bhqnk,bhqnkd->bqhd', p, vg)


def moba_attention(q, k, v, rel_bias, q_pos0):
    bsz, seq_len = k.shape[:2]
    n_blk = -(-seq_len // MOBA_BLOCK)
    pad = ((0, 0), (0, n_blk * MOBA_BLOCK - seq_len), (0, 0), (0, 0))
    to_blocks = lambda z: jnp.pad(z, pad).reshape(bsz, n_blk, MOBA_BLOCK, C_HEADS, HEAD_DIM).transpose(0, 3, 1, 2, 4)
    kb = to_blocks(k)
    vb = to_blocks(v)
    k_means = jnp.mean(kb.astype(jnp.float32), axis=3).astype(k.dtype)
    return sweep_queries(lambda qb, pos: moba_block(qb, kb, vb, k_means, rel_bias, pos), q, q_pos0, MOBA_Q_BLOCK)


def trunk_layer(x, past, w_in, g_pre, g_post, g_sgu, w_s, b_s, b_f, w_oa, w_ob, w_oc, w_o, rel_bias):
    bsz, t, _ = x.shape
    xn = rmsnorm(x, g_pre)
    (a_u, a_v, a_z, b_q, b_k, b_v, b_fl, b_z,
     c_q, c_k, c_v, c_z, g_a, g_b, g_c) = split_cols(xn @ w_in)
    heads = lambda z, h: z.reshape(bsz, t, h, HEAD_DIM)
    u = jax.nn.gelu(a_u)
    vn = rmsnorm(jax.nn.gelu(a_v), g_sgu).reshape(bsz, t, A_GROUPS, A_GROUP_DIM)
    y_a = u * chunk_spatial_mix(vn, w_s, b_s).reshape(bsz, t, A_WIDTH) * jax.nn.silu(a_z)
    qb, kb, vb = heads(b_q, B_HEADS), heads(b_k, B_HEADS), heads(b_v, B_HEADS)
    logf = jax.nn.log_sigmoid(b_fl.astype(jnp.float32) + b_f)
    qc, kc, vc = heads(c_q, C_HEADS), heads(c_k, C_HEADS), heads(c_v, C_HEADS)
    if past is None:
        pos0 = 0
        kb_all, vb_all, logf_all, kc_all, vc_all = kb, vb, logf, kc, vc
    else:
        p_fk, p_fv, p_flogf, p_mk, p_mv = past
        pos0 = p_fk.shape[1]
        kb_all = jnp.concatenate([p_fk, kb], axis=1)
        vb_all = jnp.concatenate([p_fv, vb], axis=1)
        logf_all = jnp.concatenate([p_flogf.astype(jnp.float32), logf], axis=1)
        kc_all = jnp.concatenate([p_mk, kc], axis=1)
        vc_all = jnp.concatenate([p_mv, vc], axis=1)
    y_b = fox_attention(qb, kb_all, vb_all, logf_all, pos0).reshape(bsz, t, B_WIDTH) * jax.nn.silu(b_z)
    y_c = moba_attention(qc, kc_all, vc_all, rel_bias, pos0).reshape(bsz, t, C_WIDTH) * jax.nn.silu(c_z)
    merged = (jax.nn.sigmoid(g_a) * (y_a @ w_oa) + jax.nn.sigmoid(g_b) * (y_b @ w_ob)
              + jax.nn.sigmoid(g_c) * (y_c @ w_oc))
    x_new = x + rmsnorm(merged @ w_o, g_post)
    return x_new, (kb, vb, logf, kc, vc, vn)


def gather_pages(cache, layer, page_table):
    g = cache[layer, page_table]
    return g.reshape((g.shape[0], g.shape[1] * g.shape[2]) + g.shape[3:])


def setup_inputs(seed: int = 0) -> dict:
    key = jax.random.key(seed)
    ks = jax.random.split(key, 20)
    nrm = jax.random.normal
    f32 = jnp.float32
    n_pages = PAST_LEN // PAGE_SIZE
    n_used = DEC_BATCH * n_pages
    n_pool = n_used + n_used // 4
    x_prompt = nrm(ks[0], (BATCH, SEQ, D_MODEL), f32)
    x_sample = nrm(ks[1], (DEC_BATCH, DEC_SEQ, D_MODEL), f32)
    cache_fox_k = nrm(ks[2], (DEPTH, n_pool, PAGE_SIZE, B_HEADS, HEAD_DIM), f32)
    cache_fox_v = nrm(ks[3], (DEPTH, n_pool, PAGE_SIZE, B_HEADS, HEAD_DIM), f32)
    cache_fox_logf = jax.nn.log_sigmoid(FORGET_BIAS_INIT + nrm(ks[4], (DEPTH, n_pool, PAGE_SIZE, B_HEADS), f32))
    cache_moba_k = nrm(ks[5], (DEPTH, n_pool, PAGE_SIZE, C_HEADS, HEAD_DIM), f32)
    cache_moba_v = nrm(ks[6], (DEPTH, n_pool, PAGE_SIZE, C_HEADS, HEAD_DIM), f32)
    page_table = jax.random.permutation(ks[7], n_pool)[:n_used].reshape(DEC_BATCH, n_pages).astype(jnp.int32)
    w_in = nrm(ks[8], (DEPTH, D_MODEL, N_IN), f32) * D_MODEL ** -0.5
    g_pre = 1.0 + 0.05 * nrm(ks[9], (DEPTH, D_MODEL), f32)
    g_post = 1.0 + 0.05 * nrm(ks[10], (DEPTH, D_MODEL), f32)
    g_sgu = 1.0 + 0.05 * nrm(ks[11], (DEPTH, A_WIDTH), f32)
    w_spatial = nrm(ks[12], (DEPTH, A_GROUPS, CHUNK, CHUNK), f32) * CHUNK ** -0.5
    b_spatial = 1.0 + 0.1 * nrm(ks[13], (DEPTH, A_GROUPS, CHUNK), f32)
    b_forget = FORGET_BIAS_INIT + 0.5 * nrm(ks[14], (DEPTH, B_HEADS), f32)
    w_out_a = nrm(ks[15], (DEPTH, A_WIDTH, D_MODEL), f32) * A_WIDTH ** -0.5
    w_out_b = nrm(ks[16], (DEPTH, B_WIDTH, D_MODEL), f32) * B_WIDTH ** -0.5
    w_out_c = nrm(ks[17], (DEPTH, C_WIDTH, D_MODEL), f32) * C_WIDTH ** -0.5
    w_out = nrm(ks[18], (DEPTH, D_MODEL, D_MODEL), f32) * D_MODEL ** -0.5
    rel_bias = 0.5 * nrm(ks[19], (N_BUCKETS, C_HEADS), f32)
    return {'x_prompt': x_prompt, 'x_sample': x_sample,
            'cache_fox_k': cache_fox_k, 'cache_fox_v': cache_fox_v, 'cache_fox_logf': cache_fox_logf,
            'cache_moba_k': cache_moba_k, 'cache_moba_v': cache_moba_v, 'page_table': page_table,
            'w_in': w_in, 'g_pre': g_pre, 'g_post': g_post, 'g_sgu': g_sgu,
            'w_spatial': w_spatial, 'b_spatial': b_spatial, 'b_forget': b_forget,
            'w_out_a': w_out_a, 'w_out_b': w_out_b, 'w_out_c': w_out_c, 'w_out': w_out,
            'rel_bias': rel_bias}


def reference(x_prompt, x_sample, cache_fox_k, cache_fox_v, cache_fox_logf, cache_moba_k, cache_moba_v,
              page_table, w_in, g_pre, g_post, g_sgu, w_spatial, b_spatial, b_forget,
              w_out_a, w_out_b, w_out_c, w_out, rel_bias):
    hp, hs = x_prompt, x_sample
    p_rows = ([], [], [], [], [])
    s_rows = ([], [], [], [], [], [])
    for l in range(DEPTH):
        lw = (w_in[l], g_pre[l], g_post[l], g_sgu[l], w_spatial[l], b_spatial[l], b_forget[l],
              w_out_a[l], w_out_b[l], w_out_c[l], w_out[l], rel_bias)
        hp, rows_p = trunk_layer(hp, None, *lw)
        past = (gather_pages(cache_fox_k, l, page_table), gather_pages(cache_fox_v, l, page_table),
                gather_pages(cache_fox_logf, l, page_table), gather_pages(cache_moba_k, l, page_table),
                gather_pages(cache_moba_v, l, page_table))
        hs, rows_s = trunk_layer(hs, past, *lw)
        for acc, r in zip(p_rows, rows_p[:5]):
            acc.append(r)
        for acc, r in zip(s_rows, rows_s):
            acc.append(r)
    return (hp, hs,
            jnp.stack(p_rows[0]), jnp.stack(p_rows[1]), jnp.stack(p_rows[2]),
            jnp.stack(p_rows[3]), jnp.stack(p_rows[4]),
            jnp.stack(s_rows[0]), jnp.stack(s_rows[1]), jnp.stack(s_rows[2]),
            jnp.stack(s_rows[3]), jnp.stack(s_rows[4]), jnp.stack(s_rows[5]))
```

```python
import functools
import math

import jax
import jax.numpy as jnp
from jax import lax
from jax.experimental import pallas as pl
from jax.experimental.pallas import tpu as pltpu

F32 = jnp.float32
BF16 = jnp.bfloat16

D_MODEL = 1024
HEAD_DIM = 64
N_HEADS = 8
WIDTH = N_HEADS * HEAD_DIM
N_PAIRS = N_HEADS // 2
ATTN_SCALE = HEAD_DIM ** -0.5
EPS = 1e-6
A_GROUPS = 4
A_GROUP_DIM = WIDTH // A_GROUPS
CHUNK = 128
MOBA_BLOCK = 256
MOBA_TOPK = 3
N_BUCKETS = 32
MAX_DISTANCE = 128
PAGE_SIZE = 128
LANES = 128
MAX_BLOCKS = 16
NEG = -1e30

ATT_TILE = 256
ROW_TILE = 512
VMEM_LIMIT = 48 * 1024 * 1024

_NT = (((1,), (1,)), ((), ()))


def _params(*sem):
    return pltpu.CompilerParams(dimension_semantics=sem, vmem_limit_bytes=VMEM_LIMIT)


def _const_spec(shape):
    zeros = (0,) * len(shape)
    return pl.BlockSpec(shape, lambda *_: zeros)


def _rmsnorm(x, g):
    return x * lax.rsqrt(jnp.mean(x * x, axis=-1, keepdims=True) + EPS) * g


def _log_sigmoid(x):
    return jnp.minimum(x, 0.0) - jnp.log1p(jnp.exp(-jnp.abs(x)))


def _dot(a, b, **kw):
    return jnp.dot(a, b, preferred_element_type=F32, **kw)


def _dot_nt(a, b):
    return lax.dot_general(a, b, _NT, preferred_element_type=F32)


def _t5_bucket(rel):
    n = jnp.maximum(rel, 0)
    max_exact = N_BUCKETS // 2
    nf = jnp.maximum(n, 1).astype(F32)
    large = max_exact + (jnp.log(nf / max_exact) / math.log(MAX_DISTANCE / max_exact)
                         * (N_BUCKETS - max_exact)).astype(jnp.int32)
    return jnp.where(n < max_exact, n, jnp.minimum(large, N_BUCKETS - 1))


def _bias_tables_kernel(rb_ref, tb_ref, far_ref, sb_ref, nb_ref, *, past_len, dec_seq):
    def lookup(rel, h):
        b = _t5_bucket(rel)
        out = jnp.zeros(rel.shape, F32)
        for k in range(N_BUCKETS):
            out = jnp.where(b == k, rb_ref[k, h], out)
        return out

    t = ATT_TILE
    diff = lax.broadcasted_iota(jnp.int32, (t, t), 0) - lax.broadcasted_iota(jnp.int32, (t, t), 1)
    key = lax.broadcasted_iota(jnp.int32, (1, past_len), 1)
    lane = lax.broadcasted_iota(jnp.int32, (1, LANES), 1)
    for h in range(N_HEADS):
        tb_ref[h, 0] = lookup(diff, h)
        tb_ref[h, 1] = lookup(diff + t, h)
        far_ref[h:h + 1, :] = lookup(jnp.full((1, LANES), 2 * t, jnp.int32), h)
        for i in range(dec_seq):
            r = i * N_HEADS + h
            sb_ref[r:r + 1, :] = lookup(past_len + i - key, h)
            nb_ref[r:r + 1, :] = lookup(i - lane, h)


def _bias_tables(rel_bias, past_len, dec_seq):
    rows = dec_seq * N_HEADS
    return pl.pallas_call(
        functools.partial(_bias_tables_kernel, past_len=past_len, dec_seq=dec_seq),
        out_shape=(jax.ShapeDtypeStruct((N_HEADS, 2, ATT_TILE, ATT_TILE), F32),
                   jax.ShapeDtypeStruct((N_HEADS, LANES), F32),
                   jax.ShapeDtypeStruct((rows, past_len), F32),
                   jax.ShapeDtypeStruct((rows, LANES), F32)),
        in_specs=[pl.BlockSpec(memory_space=pltpu.SMEM)],
        compiler_params=pltpu.CompilerParams(vmem_limit_bytes=VMEM_LIMIT),
        name="bias_tables",
    )(rel_bias)


def _proj_a_kernel(x_ref, gpre_ref, w_ref, gsgu_ref, ws_ref, bs_ref, ya_ref, *vn_refs, chunk):
    xn = _rmsnorm(x_ref[...], gpre_ref[...]).astype(BF16)
    p = _dot(xn, w_ref[...])
    u = jax.nn.gelu(p[:, :WIDTH])
    vn = _rmsnorm(jax.nn.gelu(p[:, WIDTH:2 * WIDTH]), gsgu_ref[...])
    z = jax.nn.silu(p[:, 2 * WIDTH:])
    if vn_refs:
        vn_refs[0][...] = vn
    vnb = vn.astype(BF16)
    causal = (lax.broadcasted_iota(jnp.int32, (chunk, chunk), 1)
              <= lax.broadcasted_iota(jnp.int32, (chunk, chunk), 0))
    for g in range(A_GROUPS):
        cols = slice(g * A_GROUP_DIM, (g + 1) * A_GROUP_DIM)
        wg = jnp.where(causal, ws_ref[g], 0.0).astype(BF16)
        bg = bs_ref[:, g:g + 1]
        for c in range(x_ref.shape[0] // chunk):
            rows = slice(c * chunk, (c + 1) * chunk)
            mix = _dot(wg, vnb[rows, cols]) + bg
            ya_ref[rows, cols] = (u[rows, cols] * mix * z[rows, cols]).astype(BF16)


def _proj_a(x, gpre, w, gsgu, ws, bs_t, *, tm, chunk, emit_vn):
    m = x.shape[0]
    row = lambda i: (i, 0)
    out_shape = [jax.ShapeDtypeStruct((m, WIDTH), BF16)]
    out_specs = [pl.BlockSpec((tm, WIDTH), row)]
    if emit_vn:
        out_shape.append(jax.ShapeDtypeStruct((m, WIDTH), F32))
        out_specs.append(pl.BlockSpec((tm, WIDTH), row))
    return pl.pallas_call(
        functools.partial(_proj_a_kernel, chunk=chunk),
        grid=(m // tm,),
        in_specs=[pl.BlockSpec((tm, D_MODEL), row), _const_spec((1, D_MODEL)),
                  _const_spec((D_MODEL, 3 * WIDTH)), _const_spec((1, WIDTH)),
                  _const_spec((A_GROUPS, chunk, chunk)), _const_spec((chunk, A_GROUPS))],
        out_specs=out_specs, out_shape=out_shape,
        compiler_params=_params("arbitrary"), name="proj_a",
    )(x, gpre, w, gsgu, ws, bs_t)


def _proj_b_kernel(x_ref, gpre_ref, w_ref, bf_ref, cm_ref,
                   q_ref, k_ref, v_ref, kb_ref, vb_ref, z_ref, lf_ref, cf_ref, carry_ref,
                   *, chunk, tiles_per_seq):
    xn = _rmsnorm(x_ref[...], gpre_ref[...]).astype(BF16)
    p = _dot(xn, w_ref[...])
    q_ref[...] = (p[:, :WIDTH] * ATTN_SCALE).astype(BF16)
    k = p[:, WIDTH:2 * WIDTH]
    v = p[:, 2 * WIDTH:3 * WIDTH]
    k_ref[...] = k
    v_ref[...] = v
    kb_ref[...] = k.astype(BF16)
    vb_ref[...] = v.astype(BF16)
    z_ref[...] = jax.nn.silu(p[:, 3 * WIDTH:4 * WIDTH]).astype(BF16)
    lf = _log_sigmoid(p[:, 4 * WIDTH:] + bf_ref[...])
    lf_ref[...] = lf[:, :N_HEADS]

    if tiles_per_seq is not None:
        @pl.when(pl.program_id(0) % tiles_per_seq == 0)
        def _():
            carry_ref[...] = jnp.zeros_like(carry_ref)
    carry = carry_ref[...] if tiles_per_seq is not None else jnp.zeros((1, LANES), F32)
    for c in range(x_ref.shape[0] // chunk):
        rows = slice(c * chunk, (c + 1) * chunk)
        cum = _dot(cm_ref[...], lf[rows], precision=lax.Precision.HIGHEST) + carry
        cf_ref[rows, :] = cum[:, :N_HEADS]
        if tiles_per_seq is not None:
            carry = cum[chunk - 1:chunk, :]
    if tiles_per_seq is not None:
        carry_ref[...] = carry


def _proj_b(x, gpre, w, bf, cm, *, tm, chunk, tiles_per_seq):
    m = x.shape[0]
    row = lambda i: (i, 0)
    wide = lambda dt: jax.ShapeDtypeStruct((m, WIDTH), dt)
    narrow = jax.ShapeDtypeStruct((m, N_HEADS), F32)
    return pl.pallas_call(
        functools.partial(_proj_b_kernel, chunk=chunk, tiles_per_seq=tiles_per_seq),
        grid=(m // tm,),
        in_specs=[pl.BlockSpec((tm, D_MODEL), row), _const_spec((1, D_MODEL)),
                  _const_spec((D_MODEL, 4 * WIDTH + LANES)), _const_spec((1, LANES)),
                  _const_spec((chunk, chunk))],
        out_specs=[pl.BlockSpec((tm, WIDTH), row)] * 6 + [pl.BlockSpec((tm, N_HEADS), row)] * 2,
        out_shape=[wide(BF16), wide(F32), wide(F32), wide(BF16), wide(BF16), wide(BF16), narrow, narrow],
        scratch_shapes=[pltpu.VMEM((1, LANES), F32)],
        compiler_params=_params("arbitrary"), name="proj_b",
    )(x, gpre, w, bf, cm)


def _proj_c_kernel(x_ref, gpre_ref, w_ref, q_ref, k_ref, v_ref, kb_ref, vb_ref, z_ref, *km_refs):
    xn = _rmsnorm(x_ref[...], gpre_ref[...]).astype(BF16)
    p = _dot(xn, w_ref[...])
    q_ref[...] = (p[:, :WIDTH] * ATTN_SCALE).astype(BF16)
    k = p[:, WIDTH:2 * WIDTH]
    v = p[:, 2 * WIDTH:3 * WIDTH]
    k_ref[...] = k
    v_ref[...] = v
    kb_ref[...] = k.astype(BF16)
    vb_ref[...] = v.astype(BF16)
    z_ref[...] = jax.nn.silu(p[:, 3 * WIDTH:]).astype(BF16)
    if km_refs:
        for c in range(x_ref.shape[0] // MOBA_BLOCK):
            rows = slice(c * MOBA_BLOCK, (c + 1) * MOBA_BLOCK)
            km_refs[0][0, c:c + 1, :] = jnp.mean(k[rows], axis=0, keepdims=True)


def _proj_c(x, gpre, w, *, tm, emit_means):
    m = x.shape[0]
    row = lambda i: (i, 0)
    wide = lambda dt: jax.ShapeDtypeStruct((m, WIDTH), dt)
    out_shape = [wide(BF16), wide(F32), wide(F32), wide(BF16), wide(BF16), wide(BF16)]
    out_specs = [pl.BlockSpec((tm, WIDTH), row)] * 6
    if emit_means:
        per = tm // MOBA_BLOCK
        out_shape.append(jax.ShapeDtypeStruct((m // tm, per, WIDTH), F32))
        out_specs.append(pl.BlockSpec((1, per, WIDTH), lambda i: (i, 0, 0)))
    return pl.pallas_call(
        _proj_c_kernel,
        grid=(m // tm,),
        in_specs=[pl.BlockSpec((tm, D_MODEL), row), _const_spec((1, D_MODEL)),
                  _const_spec((D_MODEL, 4 * WIDTH))],
        out_specs=out_specs, out_shape=out_shape,
        compiler_params=_params("arbitrary"), name="proj_c",
    )(x, gpre, w)


def _out_kernel(x_ref, ya_ref, yb_ref, yc_ref, gpre_ref, wg_ref, woa_ref, wob_ref, woc_ref, wo_ref,
                gpost_ref, o_ref):
    x = x_ref[...]
    xn = _rmsnorm(x, gpre_ref[...]).astype(BF16)
    gates = jax.nn.sigmoid(_dot(xn, wg_ref[...]))
    merged = (gates[:, :D_MODEL] * _dot(ya_ref[...], woa_ref[...])
              + gates[:, D_MODEL:2 * D_MODEL] * _dot(yb_ref[...], wob_ref[...])
              + gates[:, 2 * D_MODEL:] * _dot(yc_ref[...], woc_ref[...]))
    r = _dot(merged.astype(BF16), wo_ref[...])
    o_ref[...] = x + _rmsnorm(r, gpost_ref[...])


def _out_proj(x, ya, yb, yc, gpre, wg, woa, wob, woc, wo, gpost, *, tm):
    m = x.shape[0]
    row = lambda i: (i, 0)
    return pl.pallas_call(
        _out_kernel,
        grid=(m // tm,),
        in_specs=[pl.BlockSpec((tm, D_MODEL), row)] + [pl.BlockSpec((tm, WIDTH), row)] * 3
                 + [_const_spec((1, D_MODEL)), _const_spec((D_MODEL, 3 * D_MODEL))]
                 + [_const_spec((WIDTH, D_MODEL))] * 3
                 + [_const_spec((D_MODEL, D_MODEL)), _const_spec((1, D_MODEL))],
        out_specs=pl.BlockSpec((tm, D_MODEL), row),
        out_shape=jax.ShapeDtypeStruct((m, D_MODEL), F32),
        compiler_params=_params("arbitrary"), name="out_proj",
    )(x, ya, yb, yc, gpre, wg, woa, wob, woc, wo, gpost)


def _pair(h):
    return slice((h // 2) * LANES, (h // 2 + 1) * LANES)


def _load_masked_q(q_ref, qm_ref):
    upper = lax.broadcasted_iota(jnp.int32, (ATT_TILE, LANES), 1) >= HEAD_DIM
    for h in range(N_HEADS):
        qp = q_ref[0, :, _pair(h)]
        keep = upper if h % 2 else jnp.logical_not(upper)
        qm_ref[h] = jnp.where(keep, qp, jnp.zeros_like(qp))


def _softmax_step(s, vp, h, m_ref, l_ref, acc_ref):
    m_prev = m_ref[h]
    m_new = jnp.maximum(m_prev, jnp.max(s, axis=-1, keepdims=True))
    alpha = jnp.exp(m_prev - m_new)
    p = jnp.exp(s - m_new)
    l_ref[h] = alpha * l_ref[h] + jnp.sum(p, axis=-1, keepdims=True)
    acc_ref[h] = alpha * acc_ref[h] + _dot(p.astype(BF16), vp)
    m_ref[h] = m_new


def _finish_attention(z_ref, y_ref, l_ref, acc_ref):
    upper = lax.broadcasted_iota(jnp.int32, (ATT_TILE, LANES), 1) >= HEAD_DIM
    for pr in range(N_PAIRS):
        lo = acc_ref[2 * pr] / l_ref[2 * pr]
        hi = acc_ref[2 * pr + 1] / l_ref[2 * pr + 1]
        cols = slice(pr * LANES, (pr + 1) * LANES)
        y_ref[0, :, cols] = (jnp.where(upper, hi, lo) * z_ref[0, :, cols].astype(F32)).astype(BF16)


def _init_softmax(m_ref, l_ref, acc_ref):
    m_ref[...] = jnp.full(m_ref.shape, NEG, F32)
    l_ref[...] = jnp.zeros(l_ref.shape, F32)
    acc_ref[...] = jnp.zeros(acc_ref.shape, F32)


def _fox_kernel(q_ref, k_ref, v_ref, fq_ref, fk_ref, z_ref, y_ref, qm_ref, m_ref, l_ref, acc_ref):
    qi = pl.program_id(1)
    t = ATT_TILE
    _load_masked_q(q_ref, qm_ref)
    _init_softmax(m_ref, l_ref, acc_ref)
    causal = lax.broadcasted_iota(jnp.int32, (t, t), 1) <= lax.broadcasted_iota(jnp.int32, (t, t), 0)

    def step(j, diagonal):
        rows = pl.ds(pl.multiple_of(j * t, t), t)
        for h in range(N_HEADS):
            s = _dot_nt(qm_ref[h], k_ref[0, rows, _pair(h)])
            s = s + (fq_ref[0, :, h:h + 1] - fk_ref[0, j, h:h + 1, :])
            if diagonal:
                s = jnp.where(causal, s, NEG)
            _softmax_step(s, v_ref[0, rows, _pair(h)], h, m_ref, l_ref, acc_ref)

    def body(j, carry):
        step(j, False)
        return carry

    lax.fori_loop(0, qi, body, 0)
    step(qi, True)
    _finish_attention(z_ref, y_ref, l_ref, acc_ref)


def _fox_attention(q, k, v, cum_f, z):
    b, t, _ = q.shape
    n = t // ATT_TILE
    fk = cum_f.reshape(b, n, ATT_TILE, N_HEADS).transpose(0, 1, 3, 2)
    tile = lambda bi, qi: (bi, qi, 0)
    full = lambda bi, qi: (bi, 0, 0)
    return pl.pallas_call(
        _fox_kernel,
        grid=(b, n),
        in_specs=[pl.BlockSpec((1, ATT_TILE, WIDTH), tile), pl.BlockSpec((1, t, WIDTH), full),
                  pl.BlockSpec((1, t, WIDTH), full), pl.BlockSpec((1, ATT_TILE, N_HEADS), tile),
                  pl.BlockSpec((1, n, N_HEADS, ATT_TILE), lambda bi, qi: (bi, 0, 0, 0)),
                  pl.BlockSpec((1, ATT_TILE, WIDTH), tile)],
        out_specs=pl.BlockSpec((1, ATT_TILE, WIDTH), tile),
        out_shape=jax.ShapeDtypeStruct((b, t, WIDTH), BF16),
        scratch_shapes=[pltpu.VMEM((N_HEADS, ATT_TILE, LANES), BF16),
                        pltpu.VMEM((N_HEADS, ATT_TILE, 1), F32), pltpu.VMEM((N_HEADS, ATT_TILE, 1), F32),
                        pltpu.VMEM((N_HEADS, ATT_TILE, LANES), F32)],
        compiler_params=_params("arbitrary", "arbitrary"), name="fox_attention",
    )(q, k, v, cum_f, fk, z)


def _moba_kernel(q_ref, k_ref, v_ref, km_ref, tb_ref, far_ref, z_ref, y_ref,
                 qm_ref, m_ref, l_ref, acc_ref, sel_ref):
    qi = pl.program_id(1)
    t = ATT_TILE
    _load_masked_q(q_ref, qm_ref)
    _init_softmax(m_ref, l_ref, acc_ref)

    blk = lax.broadcasted_iota(jnp.int32, (MAX_BLOCKS, t), 0).astype(F32)
    picked = []
    for h in range(N_HEADS):
        g = _dot_nt(km_ref[0, :, _pair(h)].astype(BF16), qm_ref[h])
        g = jnp.where(blk < qi.astype(F32), g, -jnp.inf)
        sel = jnp.zeros((MAX_BLOCKS, t), F32)
        for _ in range(MOBA_TOPK):
            best = jnp.max(g, axis=0, keepdims=True)
            cand = jnp.logical_and(g == best, best > -jnp.inf)
            first = jnp.min(jnp.where(cand, blk, float(MAX_BLOCKS)), axis=0, keepdims=True)
            pick = blk == first
            sel = jnp.where(pick, 1.0, sel)
            g = jnp.where(pick, -jnp.inf, g)
        picked.append(sel)
    sel_ref[...] = jnp.concatenate(picked, axis=0).T.astype(BF16)

    causal = lax.broadcasted_iota(jnp.int32, (t, t), 1) <= lax.broadcasted_iota(jnp.int32, (t, t), 0)
    sel_row = lax.broadcasted_iota(jnp.int32, (LANES, LANES), 0)

    def step(j, kind):
        rows = pl.ds(pl.multiple_of(j * t, t), t)
        for h in range(N_HEADS):
            s = _dot_nt(qm_ref[h], k_ref[0, rows, _pair(h)])
            if kind == "own":
                s = jnp.where(causal, s + tb_ref[h, 0], NEG)
            else:
                onehot = (sel_row == h * MAX_BLOCKS + j).astype(BF16)
                chosen = _dot(sel_ref[...], onehot)
                chosen = jnp.concatenate([chosen] * (t // LANES), axis=1) > 0.5
                bias = tb_ref[h, 1] if kind == "previous" else far_ref[h:h + 1, 0:1]
                s = jnp.where(chosen, s + bias, NEG)
            _softmax_step(s, v_ref[0, rows, _pair(h)], h, m_ref, l_ref, acc_ref)

    def body(j, carry):
        step(j, "far")
        return carry

    lax.fori_loop(0, jnp.maximum(qi - 1, 0), body, 0)

    @pl.when(qi >= 1)
    def _():
        step(qi - 1, "previous")

    step(qi, "own")
    _finish_attention(z_ref, y_ref, l_ref, acc_ref)


def _moba_attention(q, k, v, k_means, tb, far, z):
    b, t, _ = q.shape
    n = t // ATT_TILE
    tile = lambda bi, qi: (bi, qi, 0)
    full = lambda bi, qi: (bi, 0, 0)
    return pl.pallas_call(
        _moba_kernel,
        grid=(b, n),
        in_specs=[pl.BlockSpec((1, ATT_TILE, WIDTH), tile), pl.BlockSpec((1, t, WIDTH), full),
                  pl.BlockSpec((1, t, WIDTH), full), pl.BlockSpec((1, MAX_BLOCKS, WIDTH), full),
                  _const_spec((N_HEADS, 2, ATT_TILE, ATT_TILE)), _const_spec((N_HEADS, LANES)),
                  pl.BlockSpec((1, ATT_TILE, WIDTH), tile)],
        out_specs=pl.BlockSpec((1, ATT_TILE, WIDTH), tile),
        out_shape=jax.ShapeDtypeStruct((b, t, WIDTH), BF16),
        scratch_shapes=[pltpu.VMEM((N_HEADS, ATT_TILE, LANES), BF16),
                        pltpu.VMEM((N_HEADS, ATT_TILE, 1), F32), pltpu.VMEM((N_HEADS, ATT_TILE, 1), F32),
                        pltpu.VMEM((N_HEADS, ATT_TILE, LANES), F32),
                        pltpu.VMEM((ATT_TILE, N_HEADS * MAX_BLOCKS), BF16)],
        compiler_params=_params("arbitrary", "arbitrary"), name="moba_attention",
    )(q, k, v, k_means, tb, far, z)


def _local_softmax(s, v):
    m = jnp.max(s, axis=-1, keepdims=True)
    p = jnp.exp(s - m)
    return m, jnp.sum(p, axis=-1, keepdims=True), _dot(p.astype(BF16), v)


def _combine(parts):
    top = None
    for m, _, _, valid in parts:
        mm = m if valid is None else jnp.where(valid, m, NEG)
        top = mm if top is None else jnp.maximum(top, mm)
    total, out = 0.0, 0.0
    for m, l, acc, valid in parts:
        w = jnp.exp(m - top)
        if valid is not None:
            w = jnp.where(valid, w, 0.0)
        total = total + w * l
        out = out + w * acc
    return out / total


def _sample_attn_kernel(pt_ref, fk_ref, fv_ref, fl_ref, mk_ref, mv_ref,
                        qf_ref, qm_ref, fkn_ref, fvn_ref, mkn_ref, mvn_ref,
                        cnc_ref, cnr_ref, sb_ref, nb_ref, fz_ref, mz_ref,
                        yb_ref, yc_ref,
                        mf_ref, lf_ref, accf_ref, mm_ref, lm_ref, accm_ref, gate_ref, ksum_ref, carry_ref,
                        *, n_pages, dec_seq):
    del pt_ref
    step = pl.program_id(1)
    page = n_pages - 1 - step
    rows = dec_seq * N_HEADS
    pages_per_block = MOBA_BLOCK // PAGE_SIZE
    lane = lax.broadcasted_iota(jnp.int32, (rows, LANES), 1)

    @pl.when(step == 0)
    def _():
        for ref in (mf_ref, lf_ref, mm_ref, lm_ref, gate_ref, ksum_ref, carry_ref):
            ref[...] = jnp.zeros(ref.shape, F32)

    qf = qf_ref[0]
    qm = qm_ref[0]

    log_f = fl_ref[0, 0]
    later = (lax.broadcasted_iota(jnp.int32, (PAGE_SIZE, PAGE_SIZE), 0)
             > lax.broadcasted_iota(jnp.int32, (PAGE_SIZE, PAGE_SIZE), 1)).astype(F32)
    suffix = _dot(log_f, later, precision=lax.Precision.HIGHEST) + carry_ref[...]
    carry_ref[...] = carry_ref[...] + jnp.sum(log_f, axis=1, keepdims=True)
    s = _dot_nt(qf, fk_ref[0, 0].astype(BF16))
    s = s + jnp.concatenate([suffix] * dec_seq, axis=0) + cnc_ref[0]
    m, l, acc = _local_softmax(s, fv_ref[0, 0].astype(BF16))
    mf_ref[...] = jnp.where(lane == page, m, mf_ref[...])
    lf_ref[...] = jnp.where(lane == page, l, lf_ref[...])
    accf_ref[page] = acc

    mk = mk_ref[0, 0]
    s = _dot_nt(qm, mk.astype(BF16)) + sb_ref[...]
    m, l, acc = _local_softmax(s, mv_ref[0, 0].astype(BF16))
    mm_ref[...] = jnp.where(lane == page, m, mm_ref[...])
    lm_ref[...] = jnp.where(lane == page, l, lm_ref[...])
    accm_ref[page] = acc
    first_of_block = page % pages_per_block == 0
    last_of_block = page % pages_per_block == pages_per_block - 1
    ksum = jnp.where(last_of_block, 0.0, ksum_ref[...]) + jnp.sum(mk, axis=0, keepdims=True)
    ksum_ref[...] = ksum
    g = jnp.sum(qm.astype(F32) * (ksum / MOBA_BLOCK), axis=1, keepdims=True)
    gate_ref[...] = jnp.where(jnp.logical_and(lane == page // pages_per_block, first_of_block),
                              g, gate_ref[...])

    @pl.when(step == n_pages - 1)
    def _():
        n_blocks = n_pages // pages_per_block
        token = lax.broadcasted_iota(jnp.int32, (rows, LANES), 0) // N_HEADS
        causal = lane <= token
        head_cols = (lax.broadcasted_iota(jnp.int32, (N_HEADS, WIDTH), 1) // HEAD_DIM
                     == lax.broadcasted_iota(jnp.int32, (N_HEADS, WIDTH), 0)).astype(F32)

        def heads_to_tokens(o):
            return jnp.sum(o.reshape(dec_seq, N_HEADS, WIDTH) * head_cols[None], axis=1)

        sn = _dot_nt(qf, fkn_ref[0]) + (cnc_ref[0] - cnr_ref[0])
        new = _local_softmax(jnp.where(causal, sn, NEG), fvn_ref[0])
        mf, lf = mf_ref[...], lf_ref[...]
        parts = [(mf[:, p:p + 1], lf[:, p:p + 1], accf_ref[p], None) for p in range(n_pages)]
        parts.append(new + (None,))
        yb_ref[0] = heads_to_tokens(_combine(parts)) * fz_ref[0].astype(F32)

        lane_f = lane.astype(F32)
        g = jnp.where(lane < n_blocks, gate_ref[...], -jnp.inf)
        sel = jnp.zeros((rows, LANES), F32)
        for _ in range(min(MOBA_TOPK, n_blocks)):
            best = jnp.max(g, axis=1, keepdims=True)
            first = jnp.min(jnp.where(g == best, lane_f, float(LANES)), axis=1, keepdims=True)
            pick = lane_f == first
            sel = jnp.where(pick, 1.0, sel)
            g = jnp.where(pick, -jnp.inf, g)
        sn = _dot_nt(qm, mkn_ref[0]) + nb_ref[...]
        new = _local_softmax(jnp.where(causal, sn, NEG), mvn_ref[0])
        mm, lm = mm_ref[...], lm_ref[...]
        parts = []
        for p in range(n_pages):
            b = p // pages_per_block
            parts.append((mm[:, p:p + 1], lm[:, p:p + 1], accm_ref[p], sel[:, b:b + 1] > 0.5))
        parts.append(new + (None,))
        yc_ref[0] = heads_to_tokens(_combine(parts)) * mz_ref[0].astype(F32)


def _sample_attention(layer, page_table, caches, qf, qm, fkn, fvn, mkn, mvn, cnc, cnr, sb, nb, fz, mz):
    cache_fk, cache_fv, cache_fl_t, cache_mk, cache_mv = caches
    nb_seq, n_pages = page_table.shape
    rows = qf.shape[1]
    dec_seq = rows // N_HEADS
    pt = page_table.reshape(-1)

    def page(bi, si, pt_ref):
        return (layer, pt_ref[bi * n_pages + n_pages - 1 - si], 0, 0)

    per_seq = lambda bi, si, pt_ref: (bi, 0, 0)
    kv_page = pl.BlockSpec((1, 1, PAGE_SIZE, WIDTH), page)
    new_kv = pl.BlockSpec((1, LANES, WIDTH), per_seq)
    out_spec = pl.BlockSpec((1, dec_seq, WIDTH), per_seq)
    stats = pltpu.VMEM((rows, LANES), F32)
    grid_spec = pltpu.PrefetchScalarGridSpec(
        num_scalar_prefetch=1,
        grid=(nb_seq, n_pages),
        in_specs=[kv_page, kv_page, pl.BlockSpec((1, 1, N_HEADS, PAGE_SIZE), page), kv_page, kv_page,
                  pl.BlockSpec((1, rows, WIDTH), per_seq), pl.BlockSpec((1, rows, WIDTH), per_seq),
                  new_kv, new_kv, new_kv, new_kv,
                  pl.BlockSpec((1, rows, 1), per_seq), pl.BlockSpec((1, rows, LANES), per_seq),
                  pl.BlockSpec((rows, PAGE_SIZE), lambda bi, si, pt_ref: (0, n_pages - 1 - si)),
                  pl.BlockSpec((rows, LANES), lambda bi, si, pt_ref: (0, 0)),
                  out_spec, out_spec],
        out_specs=[out_spec, out_spec],
        scratch_shapes=[stats, stats, pltpu.VMEM((n_pages, rows, WIDTH), F32),
                        stats, stats, pltpu.VMEM((n_pages, rows, WIDTH), F32),
                        stats, pltpu.VMEM((1, WIDTH), F32), pltpu.VMEM((N_HEADS, 1), F32)],
    )
    return pl.pallas_call(
        functools.partial(_sample_attn_kernel, n_pages=n_pages, dec_seq=dec_seq),
        grid_spec=grid_spec,
        out_shape=[jax.ShapeDtypeStruct((nb_seq, dec_seq, WIDTH), F32)] * 2,
        compiler_params=_params("arbitrary", "arbitrary"), name="sample_attention",
    )(pt, cache_fk, cache_fv, cache_fl_t, cache_mk, cache_mv,
      qf, qm, fkn, fvn, mkn, mvn, cnc, cnr, sb, nb, fz, mz)


def _layer_weights(w_in, g_pre, g_post, g_sgu, w_s, b_s, b_f, w_oa, w_ob, w_oc, w_o):
    wa = w_in[:, :3 * WIDTH]
    o = 3 * WIDTH
    wb_qkv = w_in[:, o:o + 3 * WIDTH]
    wb_f = w_in[:, o + 3 * WIDTH:o + 3 * WIDTH + N_HEADS]
    wb_z = w_in[:, o + 3 * WIDTH + N_HEADS:o + 4 * WIDTH + N_HEADS]
    o = o + 4 * WIDTH + N_HEADS
    wc = w_in[:, o:o + 4 * WIDTH]
    wg = w_in[:, o + 4 * WIDTH:]
    wb = jnp.concatenate([wb_qkv, wb_z, jnp.pad(wb_f, ((0, 0), (0, LANES - N_HEADS)))], axis=1)
    return dict(
        wa=wa.astype(BF16), wb=wb.astype(BF16), wc=wc.astype(BF16), wg=wg.astype(BF16),
        g_pre=g_pre[None, :], g_post=g_post[None, :], g_sgu=g_sgu[None, :],
        w_s=w_s, b_s=b_s, b_f=jnp.pad(b_f, (0, LANES - N_HEADS))[None, :],
        w_oa=w_oa.astype(BF16), w_ob=w_ob.astype(BF16), w_oc=w_oc.astype(BF16), w_o=w_o.astype(BF16))


def _prompt_layer(x, w, tb, far):
    b, t, _ = x.shape
    tm = min(ROW_TILE, t)
    x2 = x.reshape(b * t, D_MODEL)
    (ya,) = _proj_a(x2, w["g_pre"], w["wa"], w["g_sgu"], w["w_s"], w["b_s"].T,
                    tm=tm, chunk=CHUNK, emit_vn=False)
    tril = jnp.tril(jnp.ones((CHUNK, CHUNK), F32))
    qb, kb, vb, kb16, vb16, zb, lf, cf = _proj_b(x2, w["g_pre"], w["wb"], w["b_f"], tril,
                                                 tm=tm, chunk=CHUNK, tiles_per_seq=t // tm)
    qc, kc, vc, kc16, vc16, zc, km = _proj_c(x2, w["g_pre"], w["wc"], tm=tm, emit_means=True)
    seq = lambda a: a.reshape(b, t, a.shape[-1])
    yb = _fox_attention(seq(qb), seq(kb16), seq(vb16), seq(cf), seq(zb))
    n_blocks = t // MOBA_BLOCK
    km = jnp.pad(km.reshape(b, n_blocks, WIDTH), ((0, 0), (0, MAX_BLOCKS - n_blocks), (0, 0)))
    yc = _moba_attention(seq(qc), seq(kc16), seq(vc16), km, tb, far, seq(zc))
    out = _out_proj(x2, ya, yb.reshape(b * t, WIDTH), yc.reshape(b * t, WIDTH), w["g_pre"], w["wg"],
                    w["w_oa"], w["w_ob"], w["w_oc"], w["w_o"], w["g_post"], tm=tm)
    heads = lambda a: a.reshape(b, t, N_HEADS, HEAD_DIM)
    return out.reshape(b, t, D_MODEL), (heads(kb), heads(vb), seq(lf), heads(kc), heads(vc))


def _sample_layer(x, layer, page_table, caches, w, sb, nb):
    s, d, _ = x.shape
    m = s * d
    rows = d * N_HEADS
    x2 = x.reshape(m, D_MODEL)
    eye = jnp.eye(s, dtype=F32)
    ws = jnp.stack([jnp.kron(eye, w["w_s"][g, :d, :d]) for g in range(A_GROUPS)])
    bs_t = jnp.tile(w["b_s"][:, :d].T, (s, 1))
    ya, vn = _proj_a(x2, w["g_pre"], w["wa"], w["g_sgu"], ws, bs_t, tm=m, chunk=m, emit_vn=True)
    seg = jnp.kron(eye, jnp.tril(jnp.ones((d, d), F32)))
    qb, kb, vb, kb16, vb16, zb, lf, cf = _proj_b(x2, w["g_pre"], w["wb"], w["b_f"], seg,
                                                 tm=m, chunk=m, tiles_per_seq=None)
    qc, kc, vc, kc16, vc16, zc = _proj_c(x2, w["g_pre"], w["wc"], tm=m, emit_means=False)

    head_cols = (jnp.arange(WIDTH)[None, :] // HEAD_DIM == jnp.arange(N_HEADS)[:, None]).astype(BF16)
    per_head = lambda q: (q.reshape(s, d, 1, WIDTH) * head_cols[None, None]).reshape(s, rows, WIDTH)
    new_rows = lambda a: jnp.pad(a.reshape(s, d, WIDTH), ((0, 0), (0, LANES - d), (0, 0)))
    cum = cf.reshape(s, d, N_HEADS)
    cnc = cum.reshape(s, rows, 1)
    cnr = jnp.pad(jnp.broadcast_to(cum.transpose(0, 2, 1)[:, None], (s, d, N_HEADS, d)).reshape(s, rows, d),
                  ((0, 0), (0, 0), (0, LANES - d)))
    seq = lambda a: a.reshape(s, d, WIDTH)
    yb, yc = _sample_attention(layer, page_table, caches, per_head(qb), per_head(qc),
                               new_rows(kb16), new_rows(vb16), new_rows(kc16), new_rows(vc16),
                               cnc, cnr, sb, nb, seq(zb), seq(zc))
    out = _out_proj(x2, ya, yb.reshape(m, WIDTH).astype(BF16), yc.reshape(m, WIDTH).astype(BF16),
                    w["g_pre"], w["wg"], w["w_oa"], w["w_ob"], w["w_oc"], w["w_o"], w["g_post"], tm=m)
    heads = lambda a: a.reshape(s, d, N_HEADS, HEAD_DIM)
    rows_out = (heads(kb), heads(vb), lf.reshape(s, d, N_HEADS), heads(kc), heads(vc),
                vn.reshape(s, d, A_GROUPS, A_GROUP_DIM))
    return out.reshape(s, d, D_MODEL), rows_out


def kernel(x_prompt, x_sample, cache_fox_k, cache_fox_v, cache_fox_logf, cache_moba_k, cache_moba_v,
           page_table, w_in, g_pre, g_post, g_sgu, w_spatial, b_spatial, b_forget,
           w_out_a, w_out_b, w_out_c, w_out, rel_bias):
    depth, n_pool = cache_fox_k.shape[:2]
    past_len = page_table.shape[1] * PAGE_SIZE
    dec_seq = x_sample.shape[1]
    assert x_prompt.shape[1] % ROW_TILE == 0 and x_prompt.shape[1] // MOBA_BLOCK <= MAX_BLOCKS
    assert past_len % MOBA_BLOCK == 0 and dec_seq <= MOBA_BLOCK and past_len // MOBA_BLOCK <= LANES

    tb, far, sb, nb = _bias_tables(rel_bias, past_len, dec_seq)
    flat = lambda c: c.reshape(depth, n_pool, PAGE_SIZE, WIDTH)
    caches = (flat(cache_fox_k), flat(cache_fox_v), cache_fox_logf.transpose(0, 1, 3, 2),
              flat(cache_moba_k), flat(cache_moba_v))

    hp, hs = x_prompt, x_sample
    p_rows, s_rows = [], []
    for l in range(depth):
        w = _layer_weights(w_in[l], g_pre[l], g_post[l], g_sgu[l], w_spatial[l], b_spatial[l], b_forget[l],
                           w_out_a[l], w_out_b[l], w_out_c[l], w_out[l])
        hp, rp = _prompt_layer(hp, w, tb, far)
        hs, rs = _sample_layer(hs, l, page_table, caches, w, sb, nb)
        p_rows.append(rp)
        s_rows.append(rs)
    stack = lambda rows, i: jnp.stack([r[i] for r in rows])
    return ((hp, hs) + tuple(stack(p_rows, i) for i in range(5)) + tuple(stack(s_rows, i) for i in range(6)))
```

```python
import functools
import math

import jax
import jax.numpy as jnp
import numpy as np
from jax import lax
from jax.experimental import pallas as pl
from jax.experimental.pallas import tpu as pltpu

F32 = jnp.float32
BF16 = jnp.bfloat16

D_MODEL = 1024
HEAD_DIM = 64
N_HEADS = 8
WIDTH = N_HEADS * HEAD_DIM
ATTN_SCALE = HEAD_DIM ** -0.5
EPS = 1e-6
A_GROUPS = 4
A_GROUP_DIM = WIDTH // A_GROUPS
CHUNK = 128
MOBA_BLOCK = 256
MOBA_TOPK = 3
N_BUCKETS = 32
MAX_DISTANCE = 128
PAGE_SIZE = 128
LANES = 128
BF16_SUBLANES = 16
MAX_BLOCKS = 16
NEG = -1e30

ATT_TILE = 256
V_ROWS = HEAD_DIM + BF16_SUBLANES
SCORE_LEAD = 4
PAGES_PER_STEP = 4
ROW_TILE = 512
VMEM_LIMIT = 48 * 1024 * 1024

_NT = (((1,), (1,)), ((), ()))


def _params(*sem):
    return pltpu.CompilerParams(dimension_semantics=sem, vmem_limit_bytes=VMEM_LIMIT)


def _const_spec(shape):
    zeros = (0,) * len(shape)
    return pl.BlockSpec(shape, lambda *_: zeros)


def _rmsnorm(x, g):
    return x * lax.rsqrt(jnp.mean(x * x, axis=-1, keepdims=True) + EPS) * g


def _log_sigmoid(x):
    return jnp.minimum(x, 0.0) - jnp.log1p(jnp.exp(-jnp.abs(x)))


def _dot(a, b, **kw):
    return jnp.dot(a, b, preferred_element_type=F32, **kw)


def _dot_nt(a, b):
    return lax.dot_general(a, b, _NT, preferred_element_type=F32)


def _t5_bucket(rel):
    n = jnp.maximum(rel, 0)
    max_exact = N_BUCKETS // 2
    nf = jnp.maximum(n, 1).astype(F32)
    large = max_exact + (jnp.log(nf / max_exact) / math.log(MAX_DISTANCE / max_exact)
                         * (N_BUCKETS - max_exact)).astype(jnp.int32)
    return jnp.where(n < max_exact, n, jnp.minimum(large, N_BUCKETS - 1))


def _bias_tables_kernel(rb_ref, tb_ref, far_ref, sb_ref, nb_ref, *, past_len, dec_seq):
    def lookup(rel, h):
        b = _t5_bucket(rel)
        out = jnp.zeros(rel.shape, F32)
        for k in range(N_BUCKETS):
            out = jnp.where(b == k, rb_ref[k, h], out)
        return out

    t = ATT_TILE
    diff = lax.broadcasted_iota(jnp.int32, (t, t), 1) - lax.broadcasted_iota(jnp.int32, (t, t), 0)
    key = lax.broadcasted_iota(jnp.int32, (1, past_len), 1)
    lane = lax.broadcasted_iota(jnp.int32, (1, LANES), 1)
    for h in range(N_HEADS):
        tb_ref[h, 0] = lookup(diff, h)
        tb_ref[h, 1] = lookup(diff + t, h)
        far_ref[h:h + 1, :] = lookup(jnp.full((1, LANES), 2 * t, jnp.int32), h)
        for i in range(dec_seq):
            r = i * N_HEADS + h
            sb_ref[r:r + 1, :] = lookup(past_len + i - key, h)
            nb_ref[r:r + 1, :] = lookup(i - lane, h)


def _bias_tables(rel_bias, past_len, dec_seq):
    rows = dec_seq * N_HEADS
    return pl.pallas_call(
        functools.partial(_bias_tables_kernel, past_len=past_len, dec_seq=dec_seq),
        out_shape=(jax.ShapeDtypeStruct((N_HEADS, 2, ATT_TILE, ATT_TILE), F32),
                   jax.ShapeDtypeStruct((N_HEADS, LANES), F32),
                   jax.ShapeDtypeStruct((rows, past_len), F32),
                   jax.ShapeDtypeStruct((rows, LANES), F32)),
        in_specs=[pl.BlockSpec(memory_space=pltpu.SMEM)],
        compiler_params=pltpu.CompilerParams(vmem_limit_bytes=VMEM_LIMIT),
        name="bias_tables",
    )(rel_bias)


def _proj_a_kernel(x_ref, gpre_ref, w_ref, gsgu_ref, ws_ref, bs_ref, ya_ref, *vn_refs, chunk):
    xn = _rmsnorm(x_ref[...], gpre_ref[...]).astype(BF16)
    p = _dot(xn, w_ref[...])
    u = jax.nn.gelu(p[:, :WIDTH])
    vn = _rmsnorm(jax.nn.gelu(p[:, WIDTH:2 * WIDTH]), gsgu_ref[...])
    z = jax.nn.silu(p[:, 2 * WIDTH:])
    if vn_refs:
        vn_refs[0][...] = vn
    vnb = vn.astype(BF16)
    causal = (lax.broadcasted_iota(jnp.int32, (chunk, chunk), 1)
              <= lax.broadcasted_iota(jnp.int32, (chunk, chunk), 0))
    for g in range(A_GROUPS):
        cols = slice(g * A_GROUP_DIM, (g + 1) * A_GROUP_DIM)
        wg = jnp.where(causal, ws_ref[g], 0.0).astype(BF16)
        bg = bs_ref[:, g:g + 1]
        for c in range(x_ref.shape[0] // chunk):
            rows = slice(c * chunk, (c + 1) * chunk)
            mix = _dot(wg, vnb[rows, cols]) + bg
            ya_ref[rows, cols] = (u[rows, cols] * mix * z[rows, cols]).astype(BF16)


def _proj_a(x, gpre, w, gsgu, ws, bs_t, *, tm, chunk, emit_vn):
    m = x.shape[0]
    row = lambda i: (i, 0)
    out_shape = [jax.ShapeDtypeStruct((m, WIDTH), BF16)]
    out_specs = [pl.BlockSpec((tm, WIDTH), row)]
    if emit_vn:
        out_shape.append(jax.ShapeDtypeStruct((m, WIDTH), F32))
        out_specs.append(pl.BlockSpec((tm, WIDTH), row))
    return pl.pallas_call(
        functools.partial(_proj_a_kernel, chunk=chunk),
        grid=(m // tm,),
        in_specs=[pl.BlockSpec((tm, D_MODEL), row), _const_spec((1, D_MODEL)),
                  _const_spec((D_MODEL, 3 * WIDTH)), _const_spec((1, WIDTH)),
                  _const_spec((A_GROUPS, chunk, chunk)), _const_spec((chunk, A_GROUPS))],
        out_specs=out_specs, out_shape=out_shape,
        compiler_params=_params("arbitrary"), name="proj_a",
    )(x, gpre, w, gsgu, ws, bs_t)


def _proj_b_kernel(x_ref, gpre_ref, w_ref, bf_ref, cm_ref,
                   q_ref, k_ref, v_ref, kb_ref, vb_ref, z_ref, lf_ref, cf_ref):
    xn = _rmsnorm(x_ref[...], gpre_ref[...]).astype(BF16)
    p = _dot(xn, w_ref[...])
    q_ref[...] = (p[:, :WIDTH] * ATTN_SCALE).astype(BF16)
    k = p[:, WIDTH:2 * WIDTH]
    v = p[:, 2 * WIDTH:3 * WIDTH]
    k_ref[...] = k
    v_ref[...] = v
    kb_ref[...] = k.astype(BF16)
    vb_ref[...] = v.astype(BF16)
    z_ref[...] = jax.nn.silu(p[:, 3 * WIDTH:4 * WIDTH]).astype(BF16)
    lf = _log_sigmoid(p[:, 4 * WIDTH:] + bf_ref[...])
    lf_ref[...] = lf[:, :N_HEADS]
    cf_ref[...] = _dot(cm_ref[...], lf, precision=lax.Precision.HIGHEST)[:, :N_HEADS]


def _proj_b(x, gpre, w, bf, cm):
    m = x.shape[0]
    wide = lambda dt: jax.ShapeDtypeStruct((m, WIDTH), dt)
    narrow = jax.ShapeDtypeStruct((m, N_HEADS), F32)
    return pl.pallas_call(
        _proj_b_kernel,
        out_shape=[wide(BF16), wide(F32), wide(F32), wide(BF16), wide(BF16), wide(BF16), narrow, narrow],
        compiler_params=pltpu.CompilerParams(vmem_limit_bytes=VMEM_LIMIT), name="proj_b",
    )(x, gpre, w, bf, cm)


def _proj_c_kernel(x_ref, gpre_ref, w_ref, q_ref, k_ref, v_ref, kb_ref, vb_ref, z_ref):
    xn = _rmsnorm(x_ref[...], gpre_ref[...]).astype(BF16)
    p = _dot(xn, w_ref[...])
    q_ref[...] = (p[:, :WIDTH] * ATTN_SCALE).astype(BF16)
    k = p[:, WIDTH:2 * WIDTH]
    v = p[:, 2 * WIDTH:3 * WIDTH]
    k_ref[...] = k
    v_ref[...] = v
    kb_ref[...] = k.astype(BF16)
    vb_ref[...] = v.astype(BF16)
    z_ref[...] = jax.nn.silu(p[:, 3 * WIDTH:]).astype(BF16)


def _proj_c(x, gpre, w):
    m = x.shape[0]
    wide = lambda dt: jax.ShapeDtypeStruct((m, WIDTH), dt)
    return pl.pallas_call(
        _proj_c_kernel,
        out_shape=[wide(BF16), wide(F32), wide(F32), wide(BF16), wide(BF16), wide(BF16)],
        compiler_params=pltpu.CompilerParams(vmem_limit_bytes=VMEM_LIMIT), name="proj_c",
    )(x, gpre, w)


def _store_attention_operands(qa, ka, pt, qa_ref, ka_ref, vta_ref, kt_ref, vt_ref):
    tm = qa.shape[0]
    for h in range(N_HEADS):
        cols = slice(h * LANES, (h + 1) * LANES)
        qa_ref[0, h] = qa[:, cols].astype(BF16)
        ka_ref[0, h] = ka[:, cols].astype(BF16)
    kt_ref[0] = pt[:WIDTH]
    vt_ref[0] = pt[WIDTH:2 * WIDTH]
    ones_row = (lax.broadcasted_iota(jnp.int32, (V_ROWS - HEAD_DIM, ATT_TILE), 0) == 0).astype(BF16)
    for h in range(N_HEADS):
        rows = slice(WIDTH + h * HEAD_DIM, WIDTH + (h + 1) * HEAD_DIM)
        for c in range(tm // ATT_TILE):
            vta_ref[0, h, c, 0:HEAD_DIM, :] = pt[rows, c * ATT_TILE:(c + 1) * ATT_TILE].astype(BF16)
            vta_ref[0, h, c, HEAD_DIM:V_ROWS, :] = ones_row


def _proj_b_prompt_kernel(x_ref, gpre_ref, wn_ref, wt_ref, bf_ref, bfc_ref, tril_ref, place_ref, ones_ref,
                          qa_ref, ka_ref, vta_ref, kt_ref, vt_ref, z_ref, lft_ref, carry_ref,
                          *, tiles_per_seq):
    xn = _rmsnorm(x_ref[...], gpre_ref[...]).astype(BF16)
    p = _dot(xn, wn_ref[...])
    pt = _dot_nt(wt_ref[...], xn)
    wide = N_HEADS * LANES
    z_ref[...] = jax.nn.silu(p[:, 2 * wide:2 * wide + WIDTH]).astype(BF16)
    lf = _log_sigmoid(p[:, 2 * wide + WIDTH:] + bf_ref[...])
    lft_ref[0] = _log_sigmoid(pt[2 * WIDTH:2 * WIDTH + N_HEADS] + bfc_ref[...])

    @pl.when(pl.program_id(0) % tiles_per_seq == 0)
    def _():
        carry_ref[...] = jnp.zeros_like(carry_ref)
    carry = carry_ref[...]
    sums = []
    for c in range(x_ref.shape[0] // CHUNK):
        cum = _dot(tril_ref[...], lf[c * CHUNK:(c + 1) * CHUNK], precision=lax.Precision.HIGHEST) + carry
        sums.append(cum)
        carry = cum[CHUNK - 1:CHUNK, :]
    carry_ref[...] = carry
    cum = jnp.concatenate(sums, axis=0)
    hi = cum.astype(BF16)
    rest = cum - hi.astype(F32)
    mid = rest.astype(BF16)
    lo = (rest - mid.astype(F32)).astype(BF16)
    aug = _dot(jnp.concatenate([hi, mid, lo], axis=1), place_ref[...]) + ones_ref[...]
    qa = p[:, :wide] * ATTN_SCALE + aug[:, :wide]
    ka = p[:, wide:2 * wide] + aug[:, wide:]
    _store_attention_operands(qa, ka, pt, qa_ref, ka_ref, vta_ref, kt_ref, vt_ref)


def _decay_placement():
    wide = N_HEADS * LANES
    place = np.zeros((3 * LANES, 2 * wide), np.float32)
    ones = np.zeros((1, 2 * wide), np.float32)
    for h in range(N_HEADS):
        for piece in range(3):
            place[piece * LANES + h, h * LANES + HEAD_DIM + 3 + piece] = 1.0
            place[piece * LANES + h, wide + h * LANES + HEAD_DIM + piece] = -1.0
            ones[0, h * LANES + HEAD_DIM + piece] = 1.0
            ones[0, wide + h * LANES + HEAD_DIM + 3 + piece] = 1.0
    return jnp.asarray(place, BF16), jnp.asarray(ones, F32)


def _attention_operand_specs(b, t, tm):
    tps = t // tm
    per = tm // ATT_TILE
    shapes = [jax.ShapeDtypeStruct((b, N_HEADS, t, LANES), BF16), jax.ShapeDtypeStruct((b, N_HEADS, t, LANES), BF16),
              jax.ShapeDtypeStruct((b, N_HEADS, t // ATT_TILE, V_ROWS, ATT_TILE), BF16),
              jax.ShapeDtypeStruct((b, WIDTH, t), F32), jax.ShapeDtypeStruct((b, WIDTH, t), F32),
              jax.ShapeDtypeStruct((b * t, WIDTH), BF16)]
    heads = pl.BlockSpec((1, N_HEADS, tm, LANES), lambda i: (i // tps, 0, i % tps, 0))
    specs = [heads, heads,
             pl.BlockSpec((1, N_HEADS, per, V_ROWS, ATT_TILE), lambda i: (i // tps, 0, i % tps, 0, 0)),
             pl.BlockSpec((1, WIDTH, tm), lambda i: (i // tps, 0, i % tps)),
             pl.BlockSpec((1, WIDTH, tm), lambda i: (i // tps, 0, i % tps)),
             pl.BlockSpec((tm, WIDTH), lambda i: (i, 0))]
    return shapes, specs


def _proj_b_prompt(x, gpre, wn, wt, bf, bf_col, b, t, *, tm):
    m = x.shape[0]
    tps = t // tm
    shapes, specs = _attention_operand_specs(b, t, tm)
    place, ones = _decay_placement()
    tril = jnp.tril(jnp.ones((CHUNK, CHUNK), F32))
    return pl.pallas_call(
        functools.partial(_proj_b_prompt_kernel, tiles_per_seq=tps),
        grid=(m // tm,),
        in_specs=[pl.BlockSpec((tm, D_MODEL), lambda i: (i, 0)), _const_spec((1, D_MODEL)),
                  _const_spec(wn.shape), _const_spec(wt.shape), _const_spec((1, LANES)),
                  _const_spec((N_HEADS, 1)), _const_spec((CHUNK, CHUNK)),
                  _const_spec(place.shape), _const_spec(ones.shape)],
        out_specs=specs + [pl.BlockSpec((1, N_HEADS, tm), lambda i: (i // tps, 0, i % tps))],
        out_shape=shapes + [jax.ShapeDtypeStruct((b, N_HEADS, t), F32)],
        scratch_shapes=[pltpu.VMEM((1, LANES), F32)],
        compiler_params=_params("arbitrary"), name="proj_b_prompt",
    )(x, gpre, wn, wt, bf, bf_col, tril, place, ones)


def _proj_c_prompt_kernel(x_ref, gpre_ref, wn_ref, wt_ref, qa_ref, ka_ref, vta_ref, kt_ref, vt_ref, z_ref, km_ref):
    xn = _rmsnorm(x_ref[...], gpre_ref[...]).astype(BF16)
    p = _dot(xn, wn_ref[...])
    pt = _dot_nt(wt_ref[...], xn)
    wide = N_HEADS * LANES
    z_ref[...] = jax.nn.silu(p[:, 2 * wide:]).astype(BF16)
    ka = p[:, wide:2 * wide]
    for c in range(x_ref.shape[0] // MOBA_BLOCK):
        km_ref[0, c:c + 1, :] = jnp.mean(ka[c * MOBA_BLOCK:(c + 1) * MOBA_BLOCK], axis=0, keepdims=True)
    _store_attention_operands(p[:, :wide] * ATTN_SCALE, ka, pt, qa_ref, ka_ref, vta_ref, kt_ref, vt_ref)


def _proj_c_prompt(x, gpre, wn, wt, b, t, *, tm):
    m = x.shape[0]
    shapes, specs = _attention_operand_specs(b, t, tm)
    per = tm // MOBA_BLOCK
    return pl.pallas_call(
        _proj_c_prompt_kernel,
        grid=(m // tm,),
        in_specs=[pl.BlockSpec((tm, D_MODEL), lambda i: (i, 0)), _const_spec((1, D_MODEL)),
                  _const_spec(wn.shape), _const_spec(wt.shape)],
        out_specs=specs + [pl.BlockSpec((1, per, N_HEADS * LANES), lambda i: (i, 0, 0))],
        out_shape=shapes + [jax.ShapeDtypeStruct((m // tm, per, N_HEADS * LANES), F32)],
        compiler_params=_params("arbitrary"), name="proj_c_prompt",
    )(x, gpre, wn, wt)


def _out_kernel(x_ref, ya_ref, yb_ref, yc_ref, gpre_ref, wg_ref, woa_ref, wob_ref, woc_ref, wo_ref,
                gpost_ref, o_ref):
    x = x_ref[...]
    xn = _rmsnorm(x, gpre_ref[...]).astype(BF16)
    gates = jax.nn.sigmoid(_dot(xn, wg_ref[...]))
    merged = (gates[:, :D_MODEL] * _dot(ya_ref[...], woa_ref[...])
              + gates[:, D_MODEL:2 * D_MODEL] * _dot(yb_ref[...], wob_ref[...])
              + gates[:, 2 * D_MODEL:] * _dot(yc_ref[...], woc_ref[...]))
    r = _dot(merged.astype(BF16), wo_ref[...])
    o_ref[...] = x + _rmsnorm(r, gpost_ref[...])


def _out_proj(x, ya, yb, yc, gpre, wg, woa, wob, woc, wo, gpost, *, tm):
    m = x.shape[0]
    row = lambda i: (i, 0)
    return pl.pallas_call(
        _out_kernel,
        grid=(m // tm,),
        in_specs=[pl.BlockSpec((tm, D_MODEL), row)] + [pl.BlockSpec((tm, WIDTH), row)] * 3
                 + [_const_spec((1, D_MODEL)), _const_spec((D_MODEL, 3 * D_MODEL))]
                 + [_const_spec((WIDTH, D_MODEL))] * 3
                 + [_const_spec((D_MODEL, D_MODEL)), _const_spec((1, D_MODEL))],
        out_specs=pl.BlockSpec((tm, D_MODEL), row),
        out_shape=jax.ShapeDtypeStruct((m, D_MODEL), F32),
        compiler_params=_params("arbitrary"), name="out_proj",
    )(x, ya, yb, yc, gpre, wg, woa, wob, woc, wo, gpost)


def _softmax_step(s, vta, h, m_ref, acc_ref):
    m_prev = m_ref[h]
    m_new = jnp.maximum(m_prev, jnp.max(s, axis=0, keepdims=True))
    p = jnp.exp(s - m_new).astype(BF16)
    acc_ref[h] = jnp.exp(m_prev - m_new) * acc_ref[h] + _dot(vta, p)
    m_ref[h] = m_new


def _finish_attention(z_ref, y_ref, acc_ref):
    outs = [acc_ref[h, 0:HEAD_DIM, :] / acc_ref[h, HEAD_DIM:HEAD_DIM + 1, :] for h in range(N_HEADS)]
    o = jnp.concatenate(outs, axis=0).T
    y_ref[0] = (o * z_ref[0].astype(F32)).astype(BF16)


def _init_softmax(m_ref, acc_ref):
    m_ref[...] = jnp.full(m_ref.shape, NEG, F32)
    acc_ref[...] = jnp.zeros(acc_ref.shape, F32)


def _key_tile(ka_ref, h, j):
    start = j * ATT_TILE if isinstance(j, int) else pl.multiple_of(j * ATT_TILE, ATT_TILE)
    return ka_ref[0, h, pl.ds(start, ATT_TILE), :]


def _causal_tile():
    t = ATT_TILE
    return lax.broadcasted_iota(jnp.int32, (t, t), 0) <= lax.broadcasted_iota(jnp.int32, (t, t), 1)


def _tile_step(qa_ref, ka_ref, vta_ref, m_ref, acc_ref, j, adjust):
    def scores(h):
        return _dot_nt(_key_tile(ka_ref, h, j), qa_ref[0, h])

    pending = [scores(h) for h in range(SCORE_LEAD)]
    for h in range(N_HEADS):
        if h + SCORE_LEAD < N_HEADS:
            pending.append(scores(h + SCORE_LEAD))
        _softmax_step(adjust(h, pending[h]), vta_ref[0, h, j], h, m_ref, acc_ref)


def _fox_kernel(qa_ref, ka_ref, vta_ref, z_ref, y_ref, m_ref, acc_ref):
    qi = pl.program_id(1)
    _init_softmax(m_ref, acc_ref)
    causal = _causal_tile()
    tile_step = functools.partial(_tile_step, qa_ref, ka_ref, vta_ref, m_ref, acc_ref)

    def body(j, carry):
        tile_step(j, lambda h, s: s)
        return carry

    lax.fori_loop(0, qi, body, 0)
    tile_step(qi, lambda h, s: jnp.where(causal, s, NEG))
    _finish_attention(z_ref, y_ref, acc_ref)


def _attention_in_specs(t):
    n = t // ATT_TILE
    return [pl.BlockSpec((1, N_HEADS, ATT_TILE, LANES), lambda bi, qi: (bi, 0, qi, 0)),
            pl.BlockSpec((1, N_HEADS, t, LANES), lambda bi, qi: (bi, 0, 0, 0)),
            pl.BlockSpec((1, N_HEADS, n, V_ROWS, ATT_TILE), lambda bi, qi: (bi, 0, 0, 0, 0))]


def _attention_scratch():
    return [pltpu.VMEM((N_HEADS, 1, ATT_TILE), F32), pltpu.VMEM((N_HEADS, V_ROWS, ATT_TILE), F32)]


def _fox_attention(qa, ka, vta, z):
    b, _, t, _ = qa.shape
    tile = lambda bi, qi: (bi, qi, 0)
    return pl.pallas_call(
        _fox_kernel,
        grid=(b, t // ATT_TILE),
        in_specs=_attention_in_specs(t) + [pl.BlockSpec((1, ATT_TILE, WIDTH), tile)],
        out_specs=pl.BlockSpec((1, ATT_TILE, WIDTH), tile),
        out_shape=jax.ShapeDtypeStruct((b, t, WIDTH), BF16),
        scratch_shapes=_attention_scratch(),
        compiler_params=_params("arbitrary", "arbitrary"), name="fox_attention",
    )(qa, ka, vta, z)


def _moba_kernel(qa_ref, ka_ref, vta_ref, km_ref, tb_ref, far_ref, z_ref, y_ref, m_ref, acc_ref, pen_ref):
    qi = pl.program_id(1)
    t = ATT_TILE
    _init_softmax(m_ref, acc_ref)

    blk = lax.broadcasted_iota(jnp.int32, (MAX_BLOCKS, t), 0).astype(F32)
    own = qi.astype(F32)
    for h in range(N_HEADS):
        g = _dot_nt(km_ref[0, h].astype(BF16), qa_ref[0, h])
        g = jnp.where(blk < own, g, -jnp.inf)
        sel = jnp.zeros((MAX_BLOCKS, t), F32)
        for _ in range(MOBA_TOPK):
            best = jnp.max(g, axis=0, keepdims=True)
            cand = jnp.logical_and(g == best, best > -jnp.inf)
            first = jnp.min(jnp.where(cand, blk, float(MAX_BLOCKS)), axis=0, keepdims=True)
            pick = blk == first
            sel = jnp.where(pick, 1.0, sel)
            g = jnp.where(pick, -jnp.inf, g)
        bias = jnp.where(blk == own - 1.0, 0.0, far_ref[h:h + 1, 0:1])
        pen_ref[h] = jnp.where(sel > 0.5, bias, NEG)

    causal = _causal_tile()
    tile_step = functools.partial(_tile_step, qa_ref, ka_ref, vta_ref, m_ref, acc_ref)

    def pen_row(h, j):
        return jnp.sum(jnp.where(blk == j.astype(F32), pen_ref[h], 0.0), axis=0, keepdims=True)

    def body(j, carry):
        tile_step(j, lambda h, s: s + pen_row(h, j))
        return carry

    lax.fori_loop(0, jnp.maximum(qi - 1, 0), body, 0)

    @pl.when(qi >= 1)
    def _():
        tile_step(qi - 1, lambda h, s: s + pen_row(h, qi - 1) + tb_ref[h, 1])

    tile_step(qi, lambda h, s: jnp.where(causal, s + tb_ref[h, 0], NEG))
    _finish_attention(z_ref, y_ref, acc_ref)


def _moba_attention(qa, ka, vta, k_means, tb, far, z):
    b, _, t, _ = qa.shape
    tile = lambda bi, qi: (bi, qi, 0)
    return pl.pallas_call(
        _moba_kernel,
        grid=(b, t // ATT_TILE),
        in_specs=_attention_in_specs(t)
                 + [pl.BlockSpec((1, N_HEADS, MAX_BLOCKS, LANES), lambda bi, qi: (bi, 0, 0, 0)),
                    _const_spec((N_HEADS, 2, ATT_TILE, ATT_TILE)), _const_spec((N_HEADS, LANES)),
                    pl.BlockSpec((1, ATT_TILE, WIDTH), tile)],
        out_specs=pl.BlockSpec((1, ATT_TILE, WIDTH), tile),
        out_shape=jax.ShapeDtypeStruct((b, t, WIDTH), BF16),
        scratch_shapes=_attention_scratch() + [pltpu.VMEM((N_HEADS, MAX_BLOCKS, ATT_TILE), F32)],
        compiler_params=_params("arbitrary", "arbitrary"), name="moba_attention",
    )(qa, ka, vta, k_means, tb, far, z)


def _local_softmax(s):
    m = jnp.max(s, axis=-1, keepdims=True)
    p = jnp.exp(s - m)
    return m, jnp.sum(p, axis=-1, keepdims=True), p.astype(BF16)


def _combine(parts):
    top = None
    for m, _, _, valid in parts:
        mm = m if valid is None else jnp.where(valid, m, NEG)
        top = mm if top is None else jnp.maximum(top, mm)
    total, out = 0.0, 0.0
    for m, l, acc, valid in parts:
        w = jnp.exp(m - top)
        if valid is not None:
            w = jnp.where(valid, w, 0.0)
        total = total + w * l
        out = out + w * acc
    return out / total


def _sample_attn_kernel(pt_ref, *refs, n_pages, dec_seq, group):
    del pt_ref
    page_refs = [refs[5 * g:5 * g + 5] for g in range(group)]
    (qf_ref, qm_ref, fkn_ref, fvn_ref, mkn_ref, mvn_ref, cnc_ref, cnr_ref, sb_ref, nb_ref, fz_ref, mz_ref,
     yb_ref, yc_ref,
     mf_ref, lf_ref, accf_ref, mm_ref, lm_ref, accm_ref, gate_ref, carry_ref) = refs[5 * group:]
    step = pl.program_id(1)
    n_steps = n_pages // group
    rows = dec_seq * N_HEADS
    pages_per_block = MOBA_BLOCK // PAGE_SIZE
    lane = lax.broadcasted_iota(jnp.int32, (rows, LANES), 1)
    pages = [n_pages - 1 - (step * group + g) for g in range(group)]

    @pl.when(step == 0)
    def _():
        for ref in (mf_ref, lf_ref, mm_ref, lm_ref, gate_ref, carry_ref):
            ref[...] = jnp.zeros(ref.shape, F32)

    qf = qf_ref[0]
    qm = qm_ref[0]
    fox_raw = [_dot(qf, page_refs[g][0][0, 0].astype(BF16)) for g in range(group)]
    moba_raw = [_dot(qm, page_refs[g][3][0, 0].astype(BF16)) for g in range(group)]

    later = (lax.broadcasted_iota(jnp.int32, (PAGE_SIZE, PAGE_SIZE), 0)
             > lax.broadcasted_iota(jnp.int32, (PAGE_SIZE, PAGE_SIZE), 1)).astype(F32)
    carry = carry_ref[...]
    mf, lf = mf_ref[...], lf_ref[...]
    for g in range(group):
        log_f = page_refs[g][2][0, 0]
        suffix = _dot(log_f, later, precision=lax.Precision.HIGHEST) + carry
        carry = carry + jnp.sum(log_f, axis=1, keepdims=True)
        s = fox_raw[g] + jnp.concatenate([suffix] * dec_seq, axis=0) + cnc_ref[0]
        m, l, p = _local_softmax(s)
        mf = jnp.where(lane == pages[g], m, mf)
        lf = jnp.where(lane == pages[g], l, lf)
        accf_ref[pages[g]] = _dot_nt(p, page_refs[g][1][0, 0].astype(BF16))
    carry_ref[...] = carry
    mf_ref[...] = mf
    lf_ref[...] = lf

    mm, lm, gate = mm_ref[...], lm_ref[...], gate_ref[...]
    for g in range(group):
        local = group - 1 - g
        m, l, p = _local_softmax(moba_raw[g] + sb_ref[:, local * PAGE_SIZE:(local + 1) * PAGE_SIZE])
        mm = jnp.where(lane == pages[g], m, mm)
        lm = jnp.where(lane == pages[g], l, lm)
        accm_ref[pages[g]] = _dot_nt(p, page_refs[g][4][0, 0].astype(BF16))
    for g in range(0, group, pages_per_block):
        total = sum(jnp.sum(moba_raw[g + k], axis=1, keepdims=True) for k in range(pages_per_block))
        gate = jnp.where(lane == pages[g] // pages_per_block, total / MOBA_BLOCK, gate)
    mm_ref[...] = mm
    lm_ref[...] = lm
    gate_ref[...] = gate

    @pl.when(step == n_steps - 1)
    def _():
        n_blocks = n_pages // pages_per_block
        token = lax.broadcasted_iota(jnp.int32, (rows, LANES), 0) // N_HEADS
        causal = lane <= token
        head_cols = (lax.broadcasted_iota(jnp.int32, (N_HEADS, WIDTH), 1) // HEAD_DIM
                     == lax.broadcasted_iota(jnp.int32, (N_HEADS, WIDTH), 0)).astype(F32)

        def heads_to_tokens(o):
            return jnp.sum(o.reshape(dec_seq, N_HEADS, WIDTH) * head_cols[None], axis=1)

        def new_tokens(q, k_ref, v_ref, extra):
            s = jnp.where(causal, _dot_nt(q, k_ref[0]) + extra, NEG)
            m, l, p = _local_softmax(s)
            return m, l, _dot(p, v_ref[0]), None

        mf, lf = mf_ref[...], lf_ref[...]
        parts = [(mf[:, p:p + 1], lf[:, p:p + 1], accf_ref[p], None) for p in range(n_pages)]
        parts.append(new_tokens(qf, fkn_ref, fvn_ref, cnc_ref[0] - cnr_ref[0]))
        yb_ref[0] = heads_to_tokens(_combine(parts)) * fz_ref[0].astype(F32)

        lane_f = lane.astype(F32)
        g = jnp.where(lane < n_blocks, gate_ref[...], -jnp.inf)
        sel = jnp.zeros((rows, LANES), F32)
        for _ in range(min(MOBA_TOPK, n_blocks)):
            best = jnp.max(g, axis=1, keepdims=True)
            first = jnp.min(jnp.where(g == best, lane_f, float(LANES)), axis=1, keepdims=True)
            pick = lane_f == first
            sel = jnp.where(pick, 1.0, sel)
            g = jnp.where(pick, -jnp.inf, g)
        mm, lm = mm_ref[...], lm_ref[...]
        parts = []
        for p in range(n_pages):
            b = p // pages_per_block
            parts.append((mm[:, p:p + 1], lm[:, p:p + 1], accm_ref[p], sel[:, b:b + 1] > 0.5))
        parts.append(new_tokens(qm, mkn_ref, mvn_ref, nb_ref[...]))
        yc_ref[0] = heads_to_tokens(_combine(parts)) * mz_ref[0].astype(F32)


def _sample_attention(layer, page_table, caches, qf, qm, fkn, fvn, mkn, mvn, cnc, cnr, sb, nb, fz, mz):
    cache_fk, cache_fv, cache_fl_t, cache_mk, cache_mv = caches
    nb_seq, n_pages = page_table.shape
    rows = qf.shape[1]
    dec_seq = rows // N_HEADS
    pt = page_table.reshape(-1)

    group = PAGES_PER_STEP
    n_steps = n_pages // group
    assert n_pages % group == 0 and group % (MOBA_BLOCK // PAGE_SIZE) == 0

    def page(g):
        return lambda bi, si, pt_ref: (layer, pt_ref[bi * n_pages + n_pages - 1 - (si * group + g)], 0, 0)

    per_seq = lambda bi, si, pt_ref: (bi, 0, 0)
    new_kv = pl.BlockSpec((1, LANES, WIDTH), per_seq)
    out_spec = pl.BlockSpec((1, dec_seq, WIDTH), per_seq)
    stats = pltpu.VMEM((rows, LANES), F32)
    page_specs, page_args = [], []
    for g in range(group):
        kv_page = pl.BlockSpec((1, 1, WIDTH, PAGE_SIZE), page(g))
        page_specs += [kv_page, kv_page, pl.BlockSpec((1, 1, N_HEADS, PAGE_SIZE), page(g)), kv_page, kv_page]
        page_args += [cache_fk, cache_fv, cache_fl_t, cache_mk, cache_mv]
    grid_spec = pltpu.PrefetchScalarGridSpec(
        num_scalar_prefetch=1,
        grid=(nb_seq, n_steps),
        in_specs=page_specs
                 + [pl.BlockSpec((1, rows, WIDTH), per_seq), pl.BlockSpec((1, rows, WIDTH), per_seq),
                    new_kv, new_kv, new_kv, new_kv,
                    pl.BlockSpec((1, rows, 1), per_seq), pl.BlockSpec((1, rows, LANES), per_seq),
                    pl.BlockSpec((rows, group * PAGE_SIZE), lambda bi, si, pt_ref: (0, n_steps - 1 - si)),
                    pl.BlockSpec((rows, LANES), lambda bi, si, pt_ref: (0, 0)),
                    out_spec, out_spec],
        out_specs=[out_spec, out_spec],
        scratch_shapes=[stats, stats, pltpu.VMEM((n_pages, rows, WIDTH), F32),
                        stats, stats, pltpu.VMEM((n_pages, rows, WIDTH), F32),
                        stats, pltpu.VMEM((N_HEADS, 1), F32)],
    )
    return pl.pallas_call(
        functools.partial(_sample_attn_kernel, n_pages=n_pages, dec_seq=dec_seq, group=group),
        grid_spec=grid_spec,
        out_shape=[jax.ShapeDtypeStruct((nb_seq, dec_seq, WIDTH), F32)] * 2,
        compiler_params=_params("arbitrary", "arbitrary"), name="sample_attention",
    )(pt, *page_args, qf, qm, fkn, fvn, mkn, mvn, cnc, cnr, sb, nb, fz, mz)


def _spread_heads(w):
    d = w.shape[0]
    return jnp.pad(w.reshape(d, N_HEADS, HEAD_DIM), ((0, 0), (0, 0), (0, LANES - HEAD_DIM))).reshape(d, N_HEADS * LANES)


def _layer_weights(w_in, g_pre, g_post, g_sgu, w_s, b_s, b_f, w_oa, w_ob, w_oc, w_o):
    cols, off = [], 0
    for size in (3 * WIDTH, WIDTH, WIDTH, WIDTH, N_HEADS, WIDTH, WIDTH, WIDTH, WIDTH, WIDTH, 3 * D_MODEL):
        cols.append(w_in[:, off:off + size])
        off += size
    wa, bq, bk, bv, bfl, bz, cq, ck, cv, cz, wg = cols
    bfl_pad = jnp.pad(bfl, ((0, 0), (0, LANES - N_HEADS)))
    bf16 = lambda a: a.astype(BF16)
    return dict(
        wa=bf16(wa), wg=bf16(wg),
        wb=bf16(jnp.concatenate([bq, bk, bv, bz, bfl_pad], axis=1)),
        wc=bf16(jnp.concatenate([cq, ck, cv, cz], axis=1)),
        wb_n=bf16(jnp.concatenate([_spread_heads(bq), _spread_heads(bk), bz, bfl_pad], axis=1)),
        wb_t=bf16(jnp.concatenate([bk, bv, bfl_pad[:, :BF16_SUBLANES]], axis=1).T),
        wc_n=bf16(jnp.concatenate([_spread_heads(cq), _spread_heads(ck), cz], axis=1)),
        wc_t=bf16(jnp.concatenate([ck, cv], axis=1).T),
        g_pre=g_pre[None, :], g_post=g_post[None, :], g_sgu=g_sgu[None, :],
        w_s=w_s, b_s=b_s, b_f=jnp.pad(b_f, (0, LANES - N_HEADS))[None, :], b_f_col=b_f[:, None],
        w_oa=bf16(w_oa), w_ob=bf16(w_ob), w_oc=bf16(w_oc), w_o=bf16(w_o))


def _prompt_layer(x, w, tb, far):
    b, t, _ = x.shape
    tm = min(ROW_TILE, t)
    x2 = x.reshape(b * t, D_MODEL)
    (ya,) = _proj_a(x2, w["g_pre"], w["wa"], w["g_sgu"], w["w_s"], w["b_s"].T,
                    tm=tm, chunk=CHUNK, emit_vn=False)
    qb, kb, vb, kbt, vbt, zb, lft = _proj_b_prompt(x2, w["g_pre"], w["wb_n"], w["wb_t"], w["b_f"], w["b_f_col"],
                                                   b, t, tm=tm)
    qc, kc, vc, kct, vct, zc, km = _proj_c_prompt(x2, w["g_pre"], w["wc_n"], w["wc_t"], b, t, tm=tm)
    seq = lambda a: a.reshape(b, t, a.shape[-1])
    yb = _fox_attention(qb, kb, vb, seq(zb))
    n_blocks = t // MOBA_BLOCK
    km = km.reshape(b, n_blocks, N_HEADS, LANES).transpose(0, 2, 1, 3)
    km = jnp.pad(km, ((0, 0), (0, 0), (0, MAX_BLOCKS - n_blocks), (0, 0)))
    yc = _moba_attention(qc, kc, vc, km, tb, far, seq(zc))
    out = _out_proj(x2, ya, yb.reshape(b * t, WIDTH), yc.reshape(b * t, WIDTH), w["g_pre"], w["wg"],
                    w["w_oa"], w["w_ob"], w["w_oc"], w["w_o"], w["g_post"], tm=tm)
    heads = lambda a: a.reshape(b, N_HEADS, HEAD_DIM, t).transpose(0, 3, 1, 2)
    return out.reshape(b, t, D_MODEL), (heads(kbt), heads(vbt), lft.transpose(0, 2, 1), heads(kct), heads(vct))


def _sample_layer(x, layer, page_table, caches, w, sb, nb):
    s, d, _ = x.shape
    m = s * d
    rows = d * N_HEADS
    x2 = x.reshape(m, D_MODEL)
    eye = jnp.eye(s, dtype=F32)
    ws = jnp.stack([jnp.kron(eye, w["w_s"][g, :d, :d]) for g in range(A_GROUPS)])
    bs_t = jnp.tile(w["b_s"][:, :d].T, (s, 1))
    ya, vn = _proj_a(x2, w["g_pre"], w["wa"], w["g_sgu"], ws, bs_t, tm=m, chunk=m, emit_vn=True)
    seg = jnp.kron(eye, jnp.tril(jnp.ones((d, d), F32)))
    qb, kb, vb, kb16, vb16, zb, lf, cf = _proj_b(x2, w["g_pre"], w["wb"], w["b_f"], seg)
    qc, kc, vc, kc16, vc16, zc = _proj_c(x2, w["g_pre"], w["wc"])

    head_cols = (jnp.arange(WIDTH)[None, :] // HEAD_DIM == jnp.arange(N_HEADS)[:, None]).astype(BF16)
    per_head = lambda q: (q.reshape(s, d, 1, WIDTH) * head_cols[None, None]).reshape(s, rows, WIDTH)
    new_rows = lambda a: jnp.pad(a.reshape(s, d, WIDTH), ((0, 0), (0, LANES - d), (0, 0)))
    cum = cf.reshape(s, d, N_HEADS)
    cnc = cum.reshape(s, rows, 1)
    cnr = jnp.pad(jnp.broadcast_to(cum.transpose(0, 2, 1)[:, None], (s, d, N_HEADS, d)).reshape(s, rows, d),
                  ((0, 0), (0, 0), (0, LANES - d)))
    seq = lambda a: a.reshape(s, d, WIDTH)
    yb, yc = _sample_attention(layer, page_table, caches, per_head(qb), per_head(qc),
                               new_rows(kb16), new_rows(vb16), new_rows(kc16), new_rows(vc16),
                               cnc, cnr, sb, nb, seq(zb), seq(zc))
    out = _out_proj(x2, ya, yb.reshape(m, WIDTH).astype(BF16), yc.reshape(m, WIDTH).astype(BF16),
                    w["g_pre"], w["wg"], w["w_oa"], w["w_ob"], w["w_oc"], w["w_o"], w["g_post"], tm=m)
    heads = lambda a: a.reshape(s, d, N_HEADS, HEAD_DIM)
    rows_out = (heads(kb), heads(vb), lf.reshape(s, d, N_HEADS), heads(kc), heads(vc),
                vn.reshape(s, d, A_GROUPS, A_GROUP_DIM))
    return out.reshape(s, d, D_MODEL), rows_out


def _cache_views(cache_fox_k, cache_fox_v, cache_fox_logf, cache_moba_k, cache_moba_v):
    depth, n_pool = cache_fox_k.shape[:2]
    kv = lambda c: c.transpose(0, 1, 3, 4, 2).reshape(depth, n_pool, WIDTH, PAGE_SIZE)
    return (kv(cache_fox_k), kv(cache_fox_v), cache_fox_logf.transpose(0, 1, 3, 2),
            kv(cache_moba_k), kv(cache_moba_v))


def kernel(x_prompt, x_sample, cache_fox_k, cache_fox_v, cache_fox_logf, cache_moba_k, cache_moba_v,
           page_table, w_in, g_pre, g_post, g_sgu, w_spatial, b_spatial, b_forget,
           w_out_a, w_out_b, w_out_c, w_out, rel_bias):
    depth = cache_fox_k.shape[0]
    past_len = page_table.shape[1] * PAGE_SIZE
    dec_seq = x_sample.shape[1]
    assert x_prompt.shape[1] % ROW_TILE == 0 and x_prompt.shape[1] // MOBA_BLOCK <= MAX_BLOCKS
    assert past_len % MOBA_BLOCK == 0 and dec_seq <= MOBA_BLOCK and past_len // MOBA_BLOCK <= LANES

    tb, far, sb, nb = _bias_tables(rel_bias, past_len, dec_seq)
    caches = _cache_views(cache_fox_k, cache_fox_v, cache_fox_logf, cache_moba_k, cache_moba_v)

    hp, hs = x_prompt, x_sample
    p_rows, s_rows = [], []
    for l in range(depth):
        w = _layer_weights(w_in[l], g_pre[l], g_post[l], g_sgu[l], w_spatial[l], b_spatial[l], b_forget[l],
                           w_out_a[l], w_out_b[l], w_out_c[l], w_out[l])
        hp, rp = _prompt_layer(hp, w, tb, far)
        hs, rs = _sample_layer(hs, l, page_table, caches, w, sb, nb)
        p_rows.append(rp)
        s_rows.append(rs)
    stack = lambda rows, i: jnp.stack([r[i] for r in rows])
    return ((hp, hs) + tuple(stack(p_rows, i) for i in range(5)) + tuple(stack(s_rows, i) for i in range(6)))
```

```python
import functools
import math

import jax
import jax.numpy as jnp
import numpy as np
from jax import lax
from jax.experimental import pallas as pl
from jax.experimental.pallas import tpu as pltpu

F32 = jnp.float32
BF16 = jnp.bfloat16

D_MODEL = 1024
HEAD_DIM = 64
N_HEADS = 8
WIDTH = N_HEADS * HEAD_DIM
ATTN_SCALE = HEAD_DIM ** -0.5
EPS = 1e-6
A_GROUPS = 4
A_GROUP_DIM = WIDTH // A_GROUPS
CHUNK = 128
MOBA_BLOCK = 256
MOBA_TOPK = 3
N_BUCKETS = 32
MAX_DISTANCE = 128
PAGE_SIZE = 128
LANES = 128
BF16_SUBLANES = 16
MAX_BLOCKS = 16
NEG = -1e30

ATT_TILE = 256
V_ROWS = HEAD_DIM + BF16_SUBLANES
SCORE_LEAD = 8
PAGES_PER_STEP = 4
ROW_TILE = 512
VMEM_LIMIT = 48 * 1024 * 1024

_NT = (((1,), (1,)), ((), ()))


def _params(*sem):
    return pltpu.CompilerParams(dimension_semantics=sem, vmem_limit_bytes=VMEM_LIMIT)


def _const_spec(shape):
    zeros = (0,) * len(shape)
    return pl.BlockSpec(shape, lambda *_: zeros)


def _rmsnorm(x, g):
    return x * lax.rsqrt(jnp.mean(x * x, axis=-1, keepdims=True) + EPS) * g


def _log_sigmoid(x):
    return jnp.minimum(x, 0.0) - jnp.log1p(jnp.exp(-jnp.abs(x)))


def _dot(a, b, **kw):
    return jnp.dot(a, b, preferred_element_type=F32, **kw)


def _dot_nt(a, b):
    return lax.dot_general(a, b, _NT, preferred_element_type=F32)


def _t5_bucket(rel):
    n = jnp.maximum(rel, 0)
    max_exact = N_BUCKETS // 2
    nf = jnp.maximum(n, 1).astype(F32)
    large = max_exact + (jnp.log(nf / max_exact) / math.log(MAX_DISTANCE / max_exact)
                         * (N_BUCKETS - max_exact)).astype(jnp.int32)
    return jnp.where(n < max_exact, n, jnp.minimum(large, N_BUCKETS - 1))


def _bias_tables_kernel(rb_ref, tb_ref, far_ref, sb_ref, nb_ref, *, past_len, dec_seq):
    def lookup(rel, h):
        b = _t5_bucket(rel)
        out = jnp.zeros(rel.shape, F32)
        for k in range(N_BUCKETS):
            out = jnp.where(b == k, rb_ref[k, h], out)
        return out

    t = ATT_TILE
    diff = lax.broadcasted_iota(jnp.int32, (t, t), 1) - lax.broadcasted_iota(jnp.int32, (t, t), 0)
    key = lax.broadcasted_iota(jnp.int32, (1, past_len), 1)
    lane = lax.broadcasted_iota(jnp.int32, (1, LANES), 1)
    for h in range(N_HEADS):
        tb_ref[h, 0] = lookup(diff, h)
        tb_ref[h, 1] = lookup(diff + t, h)
        far_ref[h:h + 1, :] = lookup(jnp.full((1, LANES), 2 * t, jnp.int32), h)
        for i in range(dec_seq):
            r = i * N_HEADS + h
            sb_ref[r:r + 1, :] = lookup(past_len + i - key, h)
            nb_ref[r:r + 1, :] = lookup(i - lane, h)


def _bias_tables(rel_bias, past_len, dec_seq):
    rows = dec_seq * N_HEADS
    return pl.pallas_call(
        functools.partial(_bias_tables_kernel, past_len=past_len, dec_seq=dec_seq),
        out_shape=(jax.ShapeDtypeStruct((N_HEADS, 2, ATT_TILE, ATT_TILE), F32),
                   jax.ShapeDtypeStruct((N_HEADS, LANES), F32),
                   jax.ShapeDtypeStruct((rows, past_len), F32),
                   jax.ShapeDtypeStruct((rows, LANES), F32)),
        in_specs=[pl.BlockSpec(memory_space=pltpu.SMEM)],
        compiler_params=pltpu.CompilerParams(vmem_limit_bytes=VMEM_LIMIT),
        name="bias_tables",
    )(rel_bias)


def _proj_a_kernel(x_ref, gpre_ref, w_ref, gsgu_ref, ws_ref, bs_ref, ya_ref, *vn_refs, chunk):
    xn = _rmsnorm(x_ref[...], gpre_ref[...]).astype(BF16)
    p = _dot(xn, w_ref[...])
    u = jax.nn.gelu(p[:, :WIDTH])
    vn = _rmsnorm(jax.nn.gelu(p[:, WIDTH:2 * WIDTH]), gsgu_ref[...])
    z = jax.nn.silu(p[:, 2 * WIDTH:])
    if vn_refs:
        vn_refs[0][...] = vn
    vnb = vn.astype(BF16)
    causal = (lax.broadcasted_iota(jnp.int32, (chunk, chunk), 1)
              <= lax.broadcasted_iota(jnp.int32, (chunk, chunk), 0))
    for g in range(A_GROUPS):
        cols = slice(g * A_GROUP_DIM, (g + 1) * A_GROUP_DIM)
        wg = jnp.where(causal, ws_ref[g], 0.0).astype(BF16)
        bg = bs_ref[:, g:g + 1]
        for c in range(x_ref.shape[0] // chunk):
            rows = slice(c * chunk, (c + 1) * chunk)
            mix = _dot(wg, vnb[rows, cols]) + bg
            ya_ref[rows, cols] = (u[rows, cols] * mix * z[rows, cols]).astype(BF16)


def _proj_a(x, gpre, w, gsgu, ws, bs_t, *, tm, chunk, emit_vn):
    m = x.shape[0]
    row = lambda i: (i, 0)
    out_shape = [jax.ShapeDtypeStruct((m, WIDTH), BF16)]
    out_specs = [pl.BlockSpec((tm, WIDTH), row)]
    if emit_vn:
        out_shape.append(jax.ShapeDtypeStruct((m, WIDTH), F32))
        out_specs.append(pl.BlockSpec((tm, WIDTH), row))
    return pl.pallas_call(
        functools.partial(_proj_a_kernel, chunk=chunk),
        grid=(m // tm,),
        in_specs=[pl.BlockSpec((tm, D_MODEL), row), _const_spec((1, D_MODEL)),
                  _const_spec((D_MODEL, 3 * WIDTH)), _const_spec((1, WIDTH)),
                  _const_spec((A_GROUPS, chunk, chunk)), _const_spec((chunk, A_GROUPS))],
        out_specs=out_specs, out_shape=out_shape,
        compiler_params=_params("arbitrary"), name="proj_a",
    )(x, gpre, w, gsgu, ws, bs_t)


def _proj_b_kernel(x_ref, gpre_ref, w_ref, bf_ref, cm_ref,
                   q_ref, k_ref, v_ref, kb_ref, vb_ref, z_ref, lf_ref, cf_ref):
    xn = _rmsnorm(x_ref[...], gpre_ref[...]).astype(BF16)
    p = _dot(xn, w_ref[...])
    q_ref[...] = (p[:, :WIDTH] * ATTN_SCALE).astype(BF16)
    k = p[:, WIDTH:2 * WIDTH]
    v = p[:, 2 * WIDTH:3 * WIDTH]
    k_ref[...] = k
    v_ref[...] = v
    kb_ref[...] = k.astype(BF16)
    vb_ref[...] = v.astype(BF16)
    z_ref[...] = jax.nn.silu(p[:, 3 * WIDTH:4 * WIDTH]).astype(BF16)
    lf = _log_sigmoid(p[:, 4 * WIDTH:] + bf_ref[...])
    lf_ref[...] = lf[:, :N_HEADS]
    cf_ref[...] = _dot(cm_ref[...], lf, precision=lax.Precision.HIGHEST)[:, :N_HEADS]


def _proj_b(x, gpre, w, bf, cm):
    m = x.shape[0]
    wide = lambda dt: jax.ShapeDtypeStruct((m, WIDTH), dt)
    narrow = jax.ShapeDtypeStruct((m, N_HEADS), F32)
    return pl.pallas_call(
        _proj_b_kernel,
        out_shape=[wide(BF16), wide(F32), wide(F32), wide(BF16), wide(BF16), wide(BF16), narrow, narrow],
        compiler_params=pltpu.CompilerParams(vmem_limit_bytes=VMEM_LIMIT), name="proj_b",
    )(x, gpre, w, bf, cm)


def _proj_c_kernel(x_ref, gpre_ref, w_ref, q_ref, k_ref, v_ref, kb_ref, vb_ref, z_ref):
    xn = _rmsnorm(x_ref[...], gpre_ref[...]).astype(BF16)
    p = _dot(xn, w_ref[...])
    q_ref[...] = (p[:, :WIDTH] * ATTN_SCALE).astype(BF16)
    k = p[:, WIDTH:2 * WIDTH]
    v = p[:, 2 * WIDTH:3 * WIDTH]
    k_ref[...] = k
    v_ref[...] = v
    kb_ref[...] = k.astype(BF16)
    vb_ref[...] = v.astype(BF16)
    z_ref[...] = jax.nn.silu(p[:, 3 * WIDTH:]).astype(BF16)


def _proj_c(x, gpre, w):
    m = x.shape[0]
    wide = lambda dt: jax.ShapeDtypeStruct((m, WIDTH), dt)
    return pl.pallas_call(
        _proj_c_kernel,
        out_shape=[wide(BF16), wide(F32), wide(F32), wide(BF16), wide(BF16), wide(BF16)],
        compiler_params=pltpu.CompilerParams(vmem_limit_bytes=VMEM_LIMIT), name="proj_c",
    )(x, gpre, w)


def _store_attention_operands(qa, ka, pt, qa_ref, ka_ref, vta_ref, kt_ref, vt_ref):
    tm = qa.shape[0]
    for h in range(N_HEADS):
        cols = slice(h * LANES, (h + 1) * LANES)
        qa_ref[0, h] = qa[:, cols].astype(BF16)
        ka_ref[0, h] = ka[:, cols].astype(BF16)
    kt_ref[0] = pt[:WIDTH]
    vt_ref[0] = pt[WIDTH:2 * WIDTH]
    ones_row = (lax.broadcasted_iota(jnp.int32, (V_ROWS - HEAD_DIM, ATT_TILE), 0) == 0).astype(BF16)
    for h in range(N_HEADS):
        rows = slice(WIDTH + h * HEAD_DIM, WIDTH + (h + 1) * HEAD_DIM)
        for c in range(tm // ATT_TILE):
            vta_ref[0, h, c, 0:HEAD_DIM, :] = pt[rows, c * ATT_TILE:(c + 1) * ATT_TILE].astype(BF16)
            vta_ref[0, h, c, HEAD_DIM:V_ROWS, :] = ones_row


def _proj_b_prompt_kernel(x_ref, gpre_ref, wn_ref, wt_ref, bf_ref, bfc_ref, tril_ref, place_ref, ones_ref,
                          qa_ref, ka_ref, vta_ref, kt_ref, vt_ref, z_ref, lft_ref, carry_ref,
                          *, tiles_per_seq):
    xn = _rmsnorm(x_ref[...], gpre_ref[...]).astype(BF16)
    p = _dot(xn, wn_ref[...])
    pt = _dot_nt(wt_ref[...], xn)
    wide = N_HEADS * LANES
    z_ref[...] = jax.nn.silu(p[:, 2 * wide:2 * wide + WIDTH]).astype(BF16)
    lf = _log_sigmoid(p[:, 2 * wide + WIDTH:] + bf_ref[...])
    lft_ref[0] = _log_sigmoid(pt[2 * WIDTH:2 * WIDTH + N_HEADS] + bfc_ref[...])

    @pl.when(pl.program_id(0) % tiles_per_seq == 0)
    def _():
        carry_ref[...] = jnp.zeros_like(carry_ref)
    carry = carry_ref[...]
    sums = []
    for c in range(x_ref.shape[0] // CHUNK):
        cum = _dot(tril_ref[...], lf[c * CHUNK:(c + 1) * CHUNK], precision=lax.Precision.HIGHEST) + carry
        sums.append(cum)
        carry = cum[CHUNK - 1:CHUNK, :]
    carry_ref[...] = carry
    cum = jnp.concatenate(sums, axis=0)
    hi = cum.astype(BF16)
    rest = cum - hi.astype(F32)
    mid = rest.astype(BF16)
    lo = (rest - mid.astype(F32)).astype(BF16)
    aug = _dot(jnp.concatenate([hi, mid, lo], axis=1), place_ref[...]) + ones_ref[...]
    qa = p[:, :wide] * ATTN_SCALE + aug[:, :wide]
    ka = p[:, wide:2 * wide] + aug[:, wide:]
    _store_attention_operands(qa, ka, pt, qa_ref, ka_ref, vta_ref, kt_ref, vt_ref)


def _decay_placement():
    wide = N_HEADS * LANES
    place = np.zeros((3 * LANES, 2 * wide), np.float32)
    ones = np.zeros((1, 2 * wide), np.float32)
    for h in range(N_HEADS):
        for piece in range(3):
            place[piece * LANES + h, h * LANES + HEAD_DIM + 3 + piece] = 1.0
            place[piece * LANES + h, wide + h * LANES + HEAD_DIM + piece] = -1.0
            ones[0, h * LANES + HEAD_DIM + piece] = 1.0
            ones[0, wide + h * LANES + HEAD_DIM + 3 + piece] = 1.0
    return jnp.asarray(place, BF16), jnp.asarray(ones, F32)


def _attention_operand_specs(b, t, tm):
    tps = t // tm
    per = tm // ATT_TILE
    shapes = [jax.ShapeDtypeStruct((b, N_HEADS, t, LANES), BF16), jax.ShapeDtypeStruct((b, N_HEADS, t, LANES), BF16),
              jax.ShapeDtypeStruct((b, N_HEADS, t // ATT_TILE, V_ROWS, ATT_TILE), BF16),
              jax.ShapeDtypeStruct((b, WIDTH, t), F32), jax.ShapeDtypeStruct((b, WIDTH, t), F32),
              jax.ShapeDtypeStruct((b * t, WIDTH), BF16)]
    heads = pl.BlockSpec((1, N_HEADS, tm, LANES), lambda i: (i // tps, 0, i % tps, 0))
    specs = [heads, heads,
             pl.BlockSpec((1, N_HEADS, per, V_ROWS, ATT_TILE), lambda i: (i // tps, 0, i % tps, 0, 0)),
             pl.BlockSpec((1, WIDTH, tm), lambda i: (i // tps, 0, i % tps)),
             pl.BlockSpec((1, WIDTH, tm), lambda i: (i // tps, 0, i % tps)),
             pl.BlockSpec((tm, WIDTH), lambda i: (i, 0))]
    return shapes, specs


def _proj_b_prompt(x, gpre, wn, wt, bf, bf_col, b, t, *, tm):
    m = x.shape[0]
    tps = t // tm
    shapes, specs = _attention_operand_specs(b, t, tm)
    place, ones = _decay_placement()
    tril = jnp.tril(jnp.ones((CHUNK, CHUNK), F32))
    return pl.pallas_call(
        functools.partial(_proj_b_prompt_kernel, tiles_per_seq=tps),
        grid=(m // tm,),
        in_specs=[pl.BlockSpec((tm, D_MODEL), lambda i: (i, 0)), _const_spec((1, D_MODEL)),
                  _const_spec(wn.shape), _const_spec(wt.shape), _const_spec((1, LANES)),
                  _const_spec((N_HEADS, 1)), _const_spec((CHUNK, CHUNK)),
                  _const_spec(place.shape), _const_spec(ones.shape)],
        out_specs=specs + [pl.BlockSpec((1, N_HEADS, tm), lambda i: (i // tps, 0, i % tps))],
        out_shape=shapes + [jax.ShapeDtypeStruct((b, N_HEADS, t), F32)],
        scratch_shapes=[pltpu.VMEM((1, LANES), F32)],
        compiler_params=_params("arbitrary"), name="proj_b_prompt",
    )(x, gpre, wn, wt, bf, bf_col, tril, place, ones)


def _proj_c_prompt_kernel(x_ref, gpre_ref, wn_ref, wt_ref, qa_ref, ka_ref, vta_ref, kt_ref, vt_ref, z_ref, km_ref):
    xn = _rmsnorm(x_ref[...], gpre_ref[...]).astype(BF16)
    p = _dot(xn, wn_ref[...])
    pt = _dot_nt(wt_ref[...], xn)
    wide = N_HEADS * LANES
    z_ref[...] = jax.nn.silu(p[:, 2 * wide:]).astype(BF16)
    ka = p[:, wide:2 * wide]
    for c in range(x_ref.shape[0] // MOBA_BLOCK):
        km_ref[0, c:c + 1, :] = jnp.mean(ka[c * MOBA_BLOCK:(c + 1) * MOBA_BLOCK], axis=0, keepdims=True)
    _store_attention_operands(p[:, :wide] * ATTN_SCALE, ka, pt, qa_ref, ka_ref, vta_ref, kt_ref, vt_ref)


def _proj_c_prompt(x, gpre, wn, wt, b, t, *, tm):
    m = x.shape[0]
    shapes, specs = _attention_operand_specs(b, t, tm)
    per = tm // MOBA_BLOCK
    return pl.pallas_call(
        _proj_c_prompt_kernel,
        grid=(m // tm,),
        in_specs=[pl.BlockSpec((tm, D_MODEL), lambda i: (i, 0)), _const_spec((1, D_MODEL)),
                  _const_spec(wn.shape), _const_spec(wt.shape)],
        out_specs=specs + [pl.BlockSpec((1, per, N_HEADS * LANES), lambda i: (i, 0, 0))],
        out_shape=shapes + [jax.ShapeDtypeStruct((m // tm, per, N_HEADS * LANES), F32)],
        compiler_params=_params("arbitrary"), name="proj_c_prompt",
    )(x, gpre, wn, wt)


def _out_kernel(x_ref, ya_ref, yb_ref, yc_ref, gpre_ref, wg_ref, woa_ref, wob_ref, woc_ref, wo_ref,
                gpost_ref, o_ref):
    x = x_ref[...]
    xn = _rmsnorm(x, gpre_ref[...]).astype(BF16)
    gates = jax.nn.sigmoid(_dot(xn, wg_ref[...]))
    merged = (gates[:, :D_MODEL] * _dot(ya_ref[...], woa_ref[...])
              + gates[:, D_MODEL:2 * D_MODEL] * _dot(yb_ref[...], wob_ref[...])
              + gates[:, 2 * D_MODEL:] * _dot(yc_ref[...], woc_ref[...]))
    r = _dot(merged.astype(BF16), wo_ref[...])
    o_ref[...] = x + _rmsnorm(r, gpost_ref[...])


def _out_proj(x, ya, yb, yc, gpre, wg, woa, wob, woc, wo, gpost, *, tm):
    m = x.shape[0]
    row = lambda i: (i, 0)
    return pl.pallas_call(
        _out_kernel,
        grid=(m // tm,),
        in_specs=[pl.BlockSpec((tm, D_MODEL), row)] + [pl.BlockSpec((tm, WIDTH), row)] * 3
                 + [_const_spec((1, D_MODEL)), _const_spec((D_MODEL, 3 * D_MODEL))]
                 + [_const_spec((WIDTH, D_MODEL))] * 3
                 + [_const_spec((D_MODEL, D_MODEL)), _const_spec((1, D_MODEL))],
        out_specs=pl.BlockSpec((tm, D_MODEL), row),
        out_shape=jax.ShapeDtypeStruct((m, D_MODEL), F32),
        compiler_params=_params("arbitrary"), name="out_proj",
    )(x, ya, yb, yc, gpre, wg, woa, wob, woc, wo, gpost)


def _softmax_step(s, vta, h, m_ref, acc_ref):
    m_prev = m_ref[h]
    m_new = jnp.maximum(m_prev, jnp.max(s, axis=0, keepdims=True))
    p = jnp.exp(s - m_new).astype(BF16)
    acc_ref[h] = jnp.exp(m_prev - m_new) * acc_ref[h] + _dot(vta, p)
    m_ref[h] = m_new


def _finish_attention(z_ref, y_ref, acc_ref):
    outs = [acc_ref[h, 0:HEAD_DIM, :] / acc_ref[h, HEAD_DIM:HEAD_DIM + 1, :] for h in range(N_HEADS)]
    o = jnp.concatenate(outs, axis=0).T
    y_ref[0] = (o * z_ref[0].astype(F32)).astype(BF16)


def _init_softmax(m_ref, acc_ref):
    m_ref[...] = jnp.full(m_ref.shape, NEG, F32)
    acc_ref[...] = jnp.zeros(acc_ref.shape, F32)


def _key_tile(ka_ref, h, j):
    start = j * ATT_TILE if isinstance(j, int) else pl.multiple_of(j * ATT_TILE, ATT_TILE)
    return ka_ref[0, h, pl.ds(start, ATT_TILE), :]


def _causal_tile():
    t = ATT_TILE
    return lax.broadcasted_iota(jnp.int32, (t, t), 0) <= lax.broadcasted_iota(jnp.int32, (t, t), 1)


def _tile_step(qa_ref, ka_ref, vta_ref, m_ref, acc_ref, j, bias=None, mask=None):
    def scores(h):
        s = _dot_nt(_key_tile(ka_ref, h, j), qa_ref[0, h])
        return s if bias is None else s + bias(h)

    pending = [scores(h) for h in range(SCORE_LEAD)]
    for h in range(N_HEADS):
        if h + SCORE_LEAD < N_HEADS:
            pending.append(scores(h + SCORE_LEAD))
        s = pending[h] if mask is None else jnp.where(mask, pending[h], NEG)
        _softmax_step(s, vta_ref[0, h, j], h, m_ref, acc_ref)


def _fox_kernel(qa_ref, ka_ref, vta_ref, z_ref, y_ref, m_ref, acc_ref):
    qi = pl.program_id(1)
    _init_softmax(m_ref, acc_ref)
    causal = _causal_tile()
    tile_step = functools.partial(_tile_step, qa_ref, ka_ref, vta_ref, m_ref, acc_ref)

    def body(j, carry):
        tile_step(j)
        return carry

    lax.fori_loop(0, qi, body, 0)
    tile_step(qi, mask=causal)
    _finish_attention(z_ref, y_ref, acc_ref)


def _attention_in_specs(t):
    n = t // ATT_TILE
    return [pl.BlockSpec((1, N_HEADS, ATT_TILE, LANES), lambda bi, qi: (bi, 0, qi, 0)),
            pl.BlockSpec((1, N_HEADS, t, LANES), lambda bi, qi: (bi, 0, 0, 0)),
            pl.BlockSpec((1, N_HEADS, n, V_ROWS, ATT_TILE), lambda bi, qi: (bi, 0, 0, 0, 0))]


def _attention_scratch():
    return [pltpu.VMEM((N_HEADS, 1, ATT_TILE), F32), pltpu.VMEM((N_HEADS, V_ROWS, ATT_TILE), F32)]


def _fox_attention(qa, ka, vta, z):
    b, _, t, _ = qa.shape
    tile = lambda bi, qi: (bi, qi, 0)
    return pl.pallas_call(
        _fox_kernel,
        grid=(b, t // ATT_TILE),
        in_specs=_attention_in_specs(t) + [pl.BlockSpec((1, ATT_TILE, WIDTH), tile)],
        out_specs=pl.BlockSpec((1, ATT_TILE, WIDTH), tile),
        out_shape=jax.ShapeDtypeStruct((b, t, WIDTH), BF16),
        scratch_shapes=_attention_scratch(),
        compiler_params=_params("arbitrary", "arbitrary"), name="fox_attention",
    )(qa, ka, vta, z)


def _moba_kernel(qa_ref, ka_ref, vta_ref, km_ref, tb_ref, far_ref, z_ref, y_ref, m_ref, acc_ref, pen_ref):
    qi = pl.program_id(1)
    t = ATT_TILE
    _init_softmax(m_ref, acc_ref)

    blk = lax.broadcasted_iota(jnp.int32, (MAX_BLOCKS, t), 0).astype(F32)
    own = qi.astype(F32)
    for h in range(N_HEADS):
        g = _dot_nt(km_ref[0, h].astype(BF16), qa_ref[0, h])
        g = jnp.where(blk < own, g, -jnp.inf)
        sel = jnp.zeros((MAX_BLOCKS, t), F32)
        for _ in range(MOBA_TOPK):
            best = jnp.max(g, axis=0, keepdims=True)
            cand = jnp.logical_and(g == best, best > -jnp.inf)
            first = jnp.min(jnp.where(cand, blk, float(MAX_BLOCKS)), axis=0, keepdims=True)
            pick = blk == first
            sel = jnp.where(pick, 1.0, sel)
            g = jnp.where(pick, -jnp.inf, g)
        bias = jnp.where(blk == own - 1.0, 0.0, far_ref[h:h + 1, 0:1])
        pen_ref[h] = jnp.where(sel > 0.5, bias, NEG)

    causal = _causal_tile()
    tile_step = functools.partial(_tile_step, qa_ref, ka_ref, vta_ref, m_ref, acc_ref)

    def body(j, carry):
        tile_step(j, bias=lambda h: pen_ref[h, pl.ds(j, 1), :])
        return carry

    lax.fori_loop(0, jnp.maximum(qi - 1, 0), body, 0)

    @pl.when(qi >= 1)
    def _():
        tile_step(qi - 1, bias=lambda h: pen_ref[h, pl.ds(qi - 1, 1), :] + tb_ref[h, 1])

    tile_step(qi, bias=lambda h: tb_ref[h, 0], mask=causal)
    _finish_attention(z_ref, y_ref, acc_ref)


def _moba_attention(qa, ka, vta, k_means, tb, far, z):
    b, _, t, _ = qa.shape
    tile = lambda bi, qi: (bi, qi, 0)
    return pl.pallas_call(
        _moba_kernel,
        grid=(b, t // ATT_TILE),
        in_specs=_attention_in_specs(t)
                 + [pl.BlockSpec((1, N_HEADS, MAX_BLOCKS, LANES), lambda bi, qi: (bi, 0, 0, 0)),
                    _const_spec((N_HEADS, 2, ATT_TILE, ATT_TILE)), _const_spec((N_HEADS, LANES)),
                    pl.BlockSpec((1, ATT_TILE, WIDTH), tile)],
        out_specs=pl.BlockSpec((1, ATT_TILE, WIDTH), tile),
        out_shape=jax.ShapeDtypeStruct((b, t, WIDTH), BF16),
        scratch_shapes=_attention_scratch() + [pltpu.VMEM((N_HEADS, MAX_BLOCKS, ATT_TILE), F32)],
        compiler_params=_params("arbitrary", "arbitrary"), name="moba_attention",
    )(qa, ka, vta, k_means, tb, far, z)


def _local_softmax(s):
    m = jnp.max(s, axis=-1, keepdims=True)
    p = jnp.exp(s - m)
    return m, jnp.sum(p, axis=-1, keepdims=True), p.astype(BF16)


def _combine(parts):
    top = None
    for m, _, _, valid in parts:
        mm = m if valid is None else jnp.where(valid, m, NEG)
        top = mm if top is None else jnp.maximum(top, mm)
    total, out = 0.0, 0.0
    for m, l, acc, valid in parts:
        w = jnp.exp(m - top)
        if valid is not None:
            w = jnp.where(valid, w, 0.0)
        total = total + w * l
        out = out + w * acc
    return out / total


def _sample_attn_kernel(pt_ref, *refs, n_pages, dec_seq, group):
    del pt_ref
    page_refs = [refs[5 * g:5 * g + 5] for g in range(group)]
    (qf_ref, qm_ref, fkn_ref, fvn_ref, mkn_ref, mvn_ref, cnc_ref, cnr_ref, sb_ref, nb_ref, fz_ref, mz_ref,
     yb_ref, yc_ref,
     mf_ref, lf_ref, accf_ref, mm_ref, lm_ref, accm_ref, gate_ref, carry_ref) = refs[5 * group:]
    step = pl.program_id(1)
    n_steps = n_pages // group
    rows = dec_seq * N_HEADS
    pages_per_block = MOBA_BLOCK // PAGE_SIZE
    lane = lax.broadcasted_iota(jnp.int32, (rows, LANES), 1)
    pages = [n_pages - 1 - (step * group + g) for g in range(group)]

    @pl.when(step == 0)
    def _():
        for ref in (mf_ref, lf_ref, mm_ref, lm_ref, gate_ref, carry_ref):
            ref[...] = jnp.zeros(ref.shape, F32)

    qf = qf_ref[0]
    qm = qm_ref[0]
    later = (lax.broadcasted_iota(jnp.int32, (PAGE_SIZE, PAGE_SIZE), 0)
             > lax.broadcasted_iota(jnp.int32, (PAGE_SIZE, PAGE_SIZE), 1)).astype(F32)
    fox_raw = [_dot(qf, page_refs[g][0][0, 0].astype(BF16)) for g in range(group)]
    moba_raw = [_dot(qm, page_refs[g][3][0, 0].astype(BF16)) for g in range(group)]
    in_page = [_dot(page_refs[g][2][0, 0], later, precision=lax.Precision.HIGHEST) for g in range(group)]

    carry = carry_ref[...]
    mf, lf = mf_ref[...], lf_ref[...]
    mm, lm, gate = mm_ref[...], lm_ref[...], gate_ref[...]
    fox_p, moba_p = [], []
    for g in range(group):
        suffix = in_page[g] + carry
        carry = carry + jnp.sum(page_refs[g][2][0, 0], axis=1, keepdims=True)
        m, l, p = _local_softmax(fox_raw[g] + jnp.concatenate([suffix] * dec_seq, axis=0) + cnc_ref[0])
        mf = jnp.where(lane == pages[g], m, mf)
        lf = jnp.where(lane == pages[g], l, lf)
        fox_p.append(p)
        local = group - 1 - g
        m, l, p = _local_softmax(moba_raw[g] + sb_ref[:, local * PAGE_SIZE:(local + 1) * PAGE_SIZE])
        mm = jnp.where(lane == pages[g], m, mm)
        lm = jnp.where(lane == pages[g], l, lm)
        moba_p.append(p)
    carry_ref[...] = carry
    mf_ref[...] = mf
    lf_ref[...] = lf

    for g in range(group):
        accf_ref[pages[g]] = _dot_nt(fox_p[g], page_refs[g][1][0, 0].astype(BF16))
        accm_ref[pages[g]] = _dot_nt(moba_p[g], page_refs[g][4][0, 0].astype(BF16))
    for g in range(0, group, pages_per_block):
        total = sum(jnp.sum(moba_raw[g + k], axis=1, keepdims=True) for k in range(pages_per_block))
        gate = jnp.where(lane == pages[g] // pages_per_block, total / MOBA_BLOCK, gate)
    mm_ref[...] = mm
    lm_ref[...] = lm
    gate_ref[...] = gate

    @pl.when(step == n_steps - 1)
    def _():
        n_blocks = n_pages // pages_per_block
        token = lax.broadcasted_iota(jnp.int32, (rows, LANES), 0) // N_HEADS
        causal = lane <= token
        head_cols = (lax.broadcasted_iota(jnp.int32, (N_HEADS, WIDTH), 1) // HEAD_DIM
                     == lax.broadcasted_iota(jnp.int32, (N_HEADS, WIDTH), 0)).astype(F32)

        def heads_to_tokens(o):
            return jnp.sum(o.reshape(dec_seq, N_HEADS, WIDTH) * head_cols[None], axis=1)

        def new_tokens(q, k_ref, v_ref, extra):
            s = jnp.where(causal, _dot_nt(q, k_ref[0]) + extra, NEG)
            m, l, p = _local_softmax(s)
            return m, l, _dot(p, v_ref[0]), None

        mf, lf = mf_ref[...], lf_ref[...]
        parts = [(mf[:, p:p + 1], lf[:, p:p + 1], accf_ref[p], None) for p in range(n_pages)]
        parts.append(new_tokens(qf, fkn_ref, fvn_ref, cnc_ref[0] - cnr_ref[0]))
        yb_ref[0] = heads_to_tokens(_combine(parts)) * fz_ref[0].astype(F32)

        lane_f = lane.astype(F32)
        g = jnp.where(lane < n_blocks, gate_ref[...], -jnp.inf)
        sel = jnp.zeros((rows, LANES), F32)
        for _ in range(min(MOBA_TOPK, n_blocks)):
            best = jnp.max(g, axis=1, keepdims=True)
            first = jnp.min(jnp.where(g == best, lane_f, float(LANES)), axis=1, keepdims=True)
            pick = lane_f == first
            sel = jnp.where(pick, 1.0, sel)
            g = jnp.where(pick, -jnp.inf, g)
        mm, lm = mm_ref[...], lm_ref[...]
        parts = []
        for p in range(n_pages):
            b = p // pages_per_block
            parts.append((mm[:, p:p + 1], lm[:, p:p + 1], accm_ref[p], sel[:, b:b + 1] > 0.5))
        parts.append(new_tokens(qm, mkn_ref, mvn_ref, nb_ref[...]))
        yc_ref[0] = heads_to_tokens(_combine(parts)) * mz_ref[0].astype(F32)


def _sample_attention(layer, page_table, caches, qf, qm, fkn, fvn, mkn, mvn, cnc, cnr, sb, nb, fz, mz):
    cache_fk, cache_fv, cache_fl_t, cache_mk, cache_mv = caches
    nb_seq, n_pages = page_table.shape
    rows = qf.shape[1]
    dec_seq = rows // N_HEADS
    pt = page_table.reshape(-1)

    group = PAGES_PER_STEP
    n_steps = n_pages // group
    assert n_pages % group == 0 and group % (MOBA_BLOCK // PAGE_SIZE) == 0

    def page(g):
        return lambda bi, si, pt_ref: (layer, pt_ref[bi * n_pages + n_pages - 1 - (si * group + g)], 0, 0)

    per_seq = lambda bi, si, pt_ref: (bi, 0, 0)
    new_kv = pl.BlockSpec((1, LANES, WIDTH), per_seq)
    out_spec = pl.BlockSpec((1, dec_seq, WIDTH), per_seq)
    stats = pltpu.VMEM((rows, LANES), F32)
    page_specs, page_args = [], []
    for g in range(group):
        kv_page = pl.BlockSpec((1, 1, WIDTH, PAGE_SIZE), page(g))
        page_specs += [kv_page, kv_page, pl.BlockSpec((1, 1, N_HEADS, PAGE_SIZE), page(g)), kv_page, kv_page]
        page_args += [cache_fk, cache_fv, cache_fl_t, cache_mk, cache_mv]
    grid_spec = pltpu.PrefetchScalarGridSpec(
        num_scalar_prefetch=1,
        grid=(nb_seq, n_steps),
        in_specs=page_specs
                 + [pl.BlockSpec((1, rows, WIDTH), per_seq), pl.BlockSpec((1, rows, WIDTH), per_seq),
                    new_kv, new_kv, new_kv, new_kv,
                    pl.BlockSpec((1, rows, 1), per_seq), pl.BlockSpec((1, rows, LANES), per_seq),
                    pl.BlockSpec((rows, group * PAGE_SIZE), lambda bi, si, pt_ref: (0, n_steps - 1 - si)),
                    pl.BlockSpec((rows, LANES), lambda bi, si, pt_ref: (0, 0)),
                    out_spec, out_spec],
        out_specs=[out_spec, out_spec],
        scratch_shapes=[stats, stats, pltpu.VMEM((n_pages, rows, WIDTH), F32),
                        stats, stats, pltpu.VMEM((n_pages, rows, WIDTH), F32),
                        stats, pltpu.VMEM((N_HEADS, 1), F32)],
    )
    return pl.pallas_call(
        functools.partial(_sample_attn_kernel, n_pages=n_pages, dec_seq=dec_seq, group=group),
        grid_spec=grid_spec,
        out_shape=[jax.ShapeDtypeStruct((nb_seq, dec_seq, WIDTH), F32)] * 2,
        compiler_params=_params("arbitrary", "arbitrary"), name="sample_attention",
    )(pt, *page_args, qf, qm, fkn, fvn, mkn, mvn, cnc, cnr, sb, nb, fz, mz)


def _spread_heads(w):
    d = w.shape[0]
    return jnp.pad(w.reshape(d, N_HEADS, HEAD_DIM), ((0, 0), (0, 0), (0, LANES - HEAD_DIM))).reshape(d, N_HEADS * LANES)


def _layer_weights(w_in, g_pre, g_post, g_sgu, w_s, b_s, b_f, w_oa, w_ob, w_oc, w_o):
    cols, off = [], 0
    for size in (3 * WIDTH, WIDTH, WIDTH, WIDTH, N_HEADS, WIDTH, WIDTH, WIDTH, WIDTH, WIDTH, 3 * D_MODEL):
        cols.append(w_in[:, off:off + size])
        off += size
    wa, bq, bk, bv, bfl, bz, cq, ck, cv, cz, wg = cols
    bfl_pad = jnp.pad(bfl, ((0, 0), (0, LANES - N_HEADS)))
    bf16 = lambda a: a.astype(BF16)
    return dict(
        wa=bf16(wa), wg=bf16(wg),
        wb=bf16(jnp.concatenate([bq, bk, bv, bz, bfl_pad], axis=1)),
        wc=bf16(jnp.concatenate([cq, ck, cv, cz], axis=1)),
        wb_n=bf16(jnp.concatenate([_spread_heads(bq), _spread_heads(bk), bz, bfl_pad], axis=1)),
        wb_t=bf16(jnp.concatenate([bk, bv, bfl_pad[:, :BF16_SUBLANES]], axis=1).T),
        wc_n=bf16(jnp.concatenate([_spread_heads(cq), _spread_heads(ck), cz], axis=1)),
        wc_t=bf16(jnp.concatenate([ck, cv], axis=1).T),
        g_pre=g_pre[None, :], g_post=g_post[None, :], g_sgu=g_sgu[None, :],
        w_s=w_s, b_s=b_s, b_f=jnp.pad(b_f, (0, LANES - N_HEADS))[None, :], b_f_col=b_f[:, None],
        w_oa=bf16(w_oa), w_ob=bf16(w_ob), w_oc=bf16(w_oc), w_o=bf16(w_o))


def _prompt_layer(x, w, tb, far):
    b, t, _ = x.shape
    tm = min(ROW_TILE, t)
    x2 = x.reshape(b * t, D_MODEL)
    (ya,) = _proj_a(x2, w["g_pre"], w["wa"], w["g_sgu"], w["w_s"], w["b_s"].T,
                    tm=tm, chunk=CHUNK, emit_vn=False)
    qb, kb, vb, kbt, vbt, zb, lft = _proj_b_prompt(x2, w["g_pre"], w["wb_n"], w["wb_t"], w["b_f"], w["b_f_col"],
                                                   b, t, tm=tm)
    qc, kc, vc, kct, vct, zc, km = _proj_c_prompt(x2, w["g_pre"], w["wc_n"], w["wc_t"], b, t, tm=tm)
    seq = lambda a: a.reshape(b, t, a.shape[-1])
    yb = _fox_attention(qb, kb, vb, seq(zb))
    n_blocks = t // MOBA_BLOCK
    km = km.reshape(b, n_blocks, N_HEADS, LANES).transpose(0, 2, 1, 3)
    km = jnp.pad(km, ((0, 0), (0, 0), (0, MAX_BLOCKS - n_blocks), (0, 0)))
    yc = _moba_attention(qc, kc, vc, km, tb, far, seq(zc))
    out = _out_proj(x2, ya, yb.reshape(b * t, WIDTH), yc.reshape(b * t, WIDTH), w["g_pre"], w["wg"],
                    w["w_oa"], w["w_ob"], w["w_oc"], w["w_o"], w["g_post"], tm=tm)
    heads = lambda a: a.reshape(b, N_HEADS, HEAD_DIM, t).transpose(0, 3, 1, 2)
    return out.reshape(b, t, D_MODEL), (heads(kbt), heads(vbt), lft.transpose(0, 2, 1), heads(kct), heads(vct))


def _sample_layer(x, layer, page_table, caches, w, sb, nb):
    s, d, _ = x.shape
    m = s * d
    rows = d * N_HEADS
    x2 = x.reshape(m, D_MODEL)
    eye = jnp.eye(s, dtype=F32)
    ws = jnp.stack([jnp.kron(eye, w["w_s"][g, :d, :d]) for g in range(A_GROUPS)])
    bs_t = jnp.tile(w["b_s"][:, :d].T, (s, 1))
    ya, vn = _proj_a(x2, w["g_pre"], w["wa"], w["g_sgu"], ws, bs_t, tm=m, chunk=m, emit_vn=True)
    seg = jnp.kron(eye, jnp.tril(jnp.ones((d, d), F32)))
    qb, kb, vb, kb16, vb16, zb, lf, cf = _proj_b(x2, w["g_pre"], w["wb"], w["b_f"], seg)
    qc, kc, vc, kc16, vc16, zc = _proj_c(x2, w["g_pre"], w["wc"])

    head_cols = (jnp.arange(WIDTH)[None, :] // HEAD_DIM == jnp.arange(N_HEADS)[:, None]).astype(BF16)
    per_head = lambda q: (q.reshape(s, d, 1, WIDTH) * head_cols[None, None]).reshape(s, rows, WIDTH)
    new_rows = lambda a: jnp.pad(a.reshape(s, d, WIDTH), ((0, 0), (0, LANES - d), (0, 0)))
    cum = cf.reshape(s, d, N_HEADS)
    cnc = cum.reshape(s, rows, 1)
    cnr = jnp.pad(jnp.broadcast_to(cum.transpose(0, 2, 1)[:, None], (s, d, N_HEADS, d)).reshape(s, rows, d),
                  ((0, 0), (0, 0), (0, LANES - d)))
    seq = lambda a: a.reshape(s, d, WIDTH)
    yb, yc = _sample_attention(layer, page_table, caches, per_head(qb), per_head(qc),
                               new_rows(kb16), new_rows(vb16), new_rows(kc16), new_rows(vc16),
                               cnc, cnr, sb, nb, seq(zb), seq(zc))
    out = _out_proj(x2, ya, yb.reshape(m, WIDTH).astype(BF16), yc.reshape(m, WIDTH).astype(BF16),
                    w["g_pre"], w["wg"], w["w_oa"], w["w_ob"], w["w_oc"], w["w_o"], w["g_post"], tm=m)
    heads = lambda a: a.reshape(s, d, N_HEADS, HEAD_DIM)
    rows_out = (heads(kb), heads(vb), lf.reshape(s, d, N_HEADS), heads(kc), heads(vc),
                vn.reshape(s, d, A_GROUPS, A_GROUP_DIM))
    return out.reshape(s, d, D_MODEL), rows_out


def _cache_views(cache_fox_k, cache_fox_v, cache_fox_logf, cache_moba_k, cache_moba_v):
    depth, n_pool = cache_fox_k.shape[:2]
    kv = lambda c: c.transpose(0, 1, 3, 4, 2).reshape(depth, n_pool, WIDTH, PAGE_SIZE)
    return (kv(cache_fox_k), kv(cache_fox_v), cache_fox_logf.transpose(0, 1, 3, 2),
            kv(cache_moba_k), kv(cache_moba_v))


def kernel(x_prompt, x_sample, cache_fox_k, cache_fox_v, cache_fox_logf, cache_moba_k, cache_moba_v,
           page_table, w_in, g_pre, g_post, g_sgu, w_spatial, b_spatial, b_forget,
           w_out_a, w_out_b, w_out_c, w_out, rel_bias):
    depth = cache_fox_k.shape[0]
    past_len = page_table.shape[1] * PAGE_SIZE
    dec_seq = x_sample.shape[1]
    assert x_prompt.shape[1] % ROW_TILE == 0 and x_prompt.shape[1] // MOBA_BLOCK <= MAX_BLOCKS
    assert past_len % MOBA_BLOCK == 0 and dec_seq <= MOBA_BLOCK and past_len // MOBA_BLOCK <= LANES

    tb, far, sb, nb = _bias_tables(rel_bias, past_len, dec_seq)
    caches = _cache_views(cache_fox_k, cache_fox_v, cache_fox_logf, cache_moba_k, cache_moba_v)

    hp, hs = x_prompt, x_sample
    p_rows, s_rows = [], []
    for l in range(depth):
        w = _layer_weights(w_in[l], g_pre[l], g_post[l], g_sgu[l], w_spatial[l], b_spatial[l], b_forget[l],
                           w_out_a[l], w_out_b[l], w_out_c[l], w_out[l])
        hp, rp = _prompt_layer(hp, w, tb, far)
        hs, rs = _sample_layer(hs, l, page_table, caches, w, sb, nb)
        p_rows.append(rp)
        s_rows.append(rs)
    stack = lambda rows, i: jnp.stack([r[i] for r in rows])
    return ((hp, hs) + tuple(stack(p_rows, i) for i in range(5)) + tuple(stack(s_rows, i) for i in range(6)))
```

```python
import functools
import math

import jax
import jax.numpy as jnp
import numpy as np
from jax import lax
from jax.experimental import pallas as pl
from jax.experimental.pallas import tpu as pltpu

F32 = jnp.float32
BF16 = jnp.bfloat16

D_MODEL = 1024
HEAD_DIM = 64
N_HEADS = 8
WIDTH = N_HEADS * HEAD_DIM
ATTN_SCALE = HEAD_DIM ** -0.5
EPS = 1e-6
A_GROUPS = 4
A_GROUP_DIM = WIDTH // A_GROUPS
CHUNK = 128
MOBA_BLOCK = 256
MOBA_TOPK = 3
N_BUCKETS = 32
MAX_DISTANCE = 128
PAGE_SIZE = 128
LANES = 128
BF16_SUBLANES = 16
MAX_BLOCKS = 16
NEG = -1e30

ATT_TILE = 256
V_ROWS = HEAD_DIM + BF16_SUBLANES
SCORE_LEAD = 8
PAGES_PER_STEP = 8
ROW_TILE = 512
VMEM_LIMIT = 48 * 1024 * 1024

_NT = (((1,), (1,)), ((), ()))


def _params(*sem):
    return pltpu.CompilerParams(dimension_semantics=sem, vmem_limit_bytes=VMEM_LIMIT)


def _const_spec(shape):
    zeros = (0,) * len(shape)
    return pl.BlockSpec(shape, lambda *_: zeros)


def _rmsnorm(x, g):
    return x * lax.rsqrt(jnp.mean(x * x, axis=-1, keepdims=True) + EPS) * g


def _log_sigmoid(x):
    return jnp.minimum(x, 0.0) - jnp.log1p(jnp.exp(-jnp.abs(x)))


def _dot(a, b, **kw):
    return jnp.dot(a, b, preferred_element_type=F32, **kw)


def _dot_nt(a, b):
    return lax.dot_general(a, b, _NT, preferred_element_type=F32)


def _t5_bucket(rel):
    n = jnp.maximum(rel, 0)
    max_exact = N_BUCKETS // 2
    nf = jnp.maximum(n, 1).astype(F32)
    large = max_exact + (jnp.log(nf / max_exact) / math.log(MAX_DISTANCE / max_exact)
                         * (N_BUCKETS - max_exact)).astype(jnp.int32)
    return jnp.where(n < max_exact, n, jnp.minimum(large, N_BUCKETS - 1))


def _bias_tables_kernel(rb_ref, tb_ref, far_ref, sb_ref, nb_ref, *, past_len, dec_seq):
    def lookup(rel, h):
        b = _t5_bucket(rel)
        out = jnp.zeros(rel.shape, F32)
        for k in range(N_BUCKETS):
            out = jnp.where(b == k, rb_ref[k, h], out)
        return out

    t = ATT_TILE
    diff = lax.broadcasted_iota(jnp.int32, (t, t), 1) - lax.broadcasted_iota(jnp.int32, (t, t), 0)
    key = lax.broadcasted_iota(jnp.int32, (1, past_len), 1)
    lane = lax.broadcasted_iota(jnp.int32, (1, LANES), 1)
    for h in range(N_HEADS):
        tb_ref[h, 0] = lookup(diff, h)
        tb_ref[h, 1] = lookup(diff + t, h)
        far_ref[h:h + 1, :] = lookup(jnp.full((1, LANES), 2 * t, jnp.int32), h)
        for i in range(dec_seq):
            r = i * N_HEADS + h
            sb_ref[r:r + 1, :] = lookup(past_len + i - key, h)
            nb_ref[r:r + 1, :] = lookup(i - lane, h)


def _bias_tables(rel_bias, past_len, dec_seq):
    rows = dec_seq * N_HEADS
    return pl.pallas_call(
        functools.partial(_bias_tables_kernel, past_len=past_len, dec_seq=dec_seq),
        out_shape=(jax.ShapeDtypeStruct((N_HEADS, 2, ATT_TILE, ATT_TILE), F32),
                   jax.ShapeDtypeStruct((N_HEADS, LANES), F32),
                   jax.ShapeDtypeStruct((rows, past_len), F32),
                   jax.ShapeDtypeStruct((rows, LANES), F32)),
        in_specs=[pl.BlockSpec(memory_space=pltpu.SMEM)],
        compiler_params=pltpu.CompilerParams(vmem_limit_bytes=VMEM_LIMIT),
        name="bias_tables",
    )(rel_bias)


def _proj_a_kernel(x_ref, gpre_ref, w_ref, gsgu_ref, ws_ref, bs_ref, ya_ref):
    xn = _rmsnorm(x_ref[...], gpre_ref[...]).astype(BF16)
    p = _dot(xn, w_ref[...])
    u = jax.nn.gelu(p[:, :WIDTH])
    vnb = _rmsnorm(jax.nn.gelu(p[:, WIDTH:2 * WIDTH]), gsgu_ref[...]).astype(BF16)
    z = jax.nn.silu(p[:, 2 * WIDTH:])
    causal = (lax.broadcasted_iota(jnp.int32, (CHUNK, CHUNK), 1)
              <= lax.broadcasted_iota(jnp.int32, (CHUNK, CHUNK), 0))
    for g in range(A_GROUPS):
        cols = slice(g * A_GROUP_DIM, (g + 1) * A_GROUP_DIM)
        wg = jnp.where(causal, ws_ref[g], 0.0).astype(BF16)
        bg = bs_ref[:, g:g + 1]
        for c in range(x_ref.shape[0] // CHUNK):
            rows = slice(c * CHUNK, (c + 1) * CHUNK)
            mix = _dot(wg, vnb[rows, cols]) + bg
            ya_ref[rows, cols] = (u[rows, cols] * mix * z[rows, cols]).astype(BF16)


def _proj_a(x, gpre, w, gsgu, ws, bs_t, *, tm):
    m = x.shape[0]
    row = lambda i: (i, 0)
    return pl.pallas_call(
        _proj_a_kernel,
        grid=(m // tm,),
        in_specs=[pl.BlockSpec((tm, D_MODEL), row), _const_spec((1, D_MODEL)),
                  _const_spec((D_MODEL, 3 * WIDTH)), _const_spec((1, WIDTH)),
                  _const_spec((A_GROUPS, CHUNK, CHUNK)), _const_spec((CHUNK, A_GROUPS))],
        out_specs=pl.BlockSpec((tm, WIDTH), row),
        out_shape=jax.ShapeDtypeStruct((m, WIDTH), BF16),
        compiler_params=_params("arbitrary"), name="proj_a",
    )(x, gpre, w, gsgu, ws, bs_t)


def _proj_a_sample_kernel(x_ref, gpre_ref, w_ref, gsgu_ref, ws_ref, bs_ref, ya_ref, vn_ref, *, dec_seq):
    xn = _rmsnorm(x_ref[...], gpre_ref[...]).astype(BF16)
    p = _dot(xn, w_ref[...])
    u = jax.nn.gelu(p[:, :WIDTH])
    vn = _rmsnorm(jax.nn.gelu(p[:, WIDTH:2 * WIDTH]), gsgu_ref[...])
    z = jax.nn.silu(p[:, 2 * WIDTH:])
    vn_ref[...] = vn
    vnb = vn.astype(BF16)
    m = x_ref.shape[0]
    r = lax.broadcasted_iota(jnp.int32, (m, m), 0)
    c = lax.broadcasted_iota(jnp.int32, (m, m), 1)
    pair = jnp.where(r // dec_seq == c // dec_seq, (r % dec_seq) * dec_seq + c % dec_seq, -1)
    token = lax.broadcasted_iota(jnp.int32, (m, 1), 0) % dec_seq
    for g in range(A_GROUPS):
        cols = slice(g * A_GROUP_DIM, (g + 1) * A_GROUP_DIM)
        wg = jnp.zeros((m, m), F32)
        bg = jnp.zeros((m, 1), F32)
        for t in range(dec_seq):
            bg = jnp.where(token == t, bs_ref[g, t], bg)
            for s in range(t + 1):
                wg = jnp.where(pair == t * dec_seq + s, ws_ref[g * dec_seq + t, s], wg)
        mix = _dot(wg.astype(BF16), vnb[:, cols]) + bg
        ya_ref[:, cols] = (u[:, cols] * mix * z[:, cols]).astype(BF16)


def _proj_a_sample(x, gpre, w, gsgu, w_s, b_s, *, dec_seq):
    m = x.shape[0]
    vmem = pl.BlockSpec(memory_space=pltpu.VMEM)
    smem = pl.BlockSpec(memory_space=pltpu.SMEM)
    return pl.pallas_call(
        functools.partial(_proj_a_sample_kernel, dec_seq=dec_seq),
        in_specs=[vmem, vmem, vmem, vmem, smem, smem],
        out_shape=[jax.ShapeDtypeStruct((m, WIDTH), BF16), jax.ShapeDtypeStruct((m, WIDTH), F32)],
        compiler_params=pltpu.CompilerParams(vmem_limit_bytes=VMEM_LIMIT), name="proj_a_sample",
    )(x, gpre, w, gsgu, w_s[:, :dec_seq, :dec_seq].reshape(A_GROUPS * dec_seq, dec_seq), b_s[:, :dec_seq])


def _proj_b_kernel(x_ref, gpre_ref, w_ref, bf_ref, cm_ref,
                   q_ref, k_ref, v_ref, kb_ref, vb_ref, z_ref, lf_ref, cf_ref):
    xn = _rmsnorm(x_ref[...], gpre_ref[...]).astype(BF16)
    p = _dot(xn, w_ref[...])
    q_ref[...] = (p[:, :WIDTH] * ATTN_SCALE).astype(BF16)
    k = p[:, WIDTH:2 * WIDTH]
    v = p[:, 2 * WIDTH:3 * WIDTH]
    k_ref[...] = k
    v_ref[...] = v
    kb_ref[...] = k.astype(BF16)
    vb_ref[...] = v.astype(BF16)
    z_ref[...] = jax.nn.silu(p[:, 3 * WIDTH:4 * WIDTH]).astype(BF16)
    lf = _log_sigmoid(p[:, 4 * WIDTH:] + bf_ref[...])
    lf_ref[...] = lf[:, :N_HEADS]
    cf_ref[...] = _dot(cm_ref[...], lf, precision=lax.Precision.HIGHEST)[:, :N_HEADS]


def _proj_b(x, gpre, w, bf, cm):
    m = x.shape[0]
    wide = lambda dt: jax.ShapeDtypeStruct((m, WIDTH), dt)
    narrow = jax.ShapeDtypeStruct((m, N_HEADS), F32)
    return pl.pallas_call(
        _proj_b_kernel,
        out_shape=[wide(BF16), wide(F32), wide(F32), wide(BF16), wide(BF16), wide(BF16), narrow, narrow],
        compiler_params=pltpu.CompilerParams(vmem_limit_bytes=VMEM_LIMIT), name="proj_b",
    )(x, gpre, w, bf, cm)


def _proj_c_kernel(x_ref, gpre_ref, w_ref, q_ref, k_ref, v_ref, kb_ref, vb_ref, z_ref):
    xn = _rmsnorm(x_ref[...], gpre_ref[...]).astype(BF16)
    p = _dot(xn, w_ref[...])
    q_ref[...] = (p[:, :WIDTH] * ATTN_SCALE).astype(BF16)
    k = p[:, WIDTH:2 * WIDTH]
    v = p[:, 2 * WIDTH:3 * WIDTH]
    k_ref[...] = k
    v_ref[...] = v
    kb_ref[...] = k.astype(BF16)
    vb_ref[...] = v.astype(BF16)
    z_ref[...] = jax.nn.silu(p[:, 3 * WIDTH:]).astype(BF16)


def _proj_c(x, gpre, w):
    m = x.shape[0]
    wide = lambda dt: jax.ShapeDtypeStruct((m, WIDTH), dt)
    return pl.pallas_call(
        _proj_c_kernel,
        out_shape=[wide(BF16), wide(F32), wide(F32), wide(BF16), wide(BF16), wide(BF16)],
        compiler_params=pltpu.CompilerParams(vmem_limit_bytes=VMEM_LIMIT), name="proj_c",
    )(x, gpre, w)


def _store_attention_operands(qa, ka, pt, qa_ref, ka_ref, vta_ref, kt_ref, vt_ref):
    tm = qa.shape[0]
    for h in range(N_HEADS):
        cols = slice(h * LANES, (h + 1) * LANES)
        qa_ref[0, h] = qa[:, cols].astype(BF16)
        ka_ref[0, h] = ka[:, cols].astype(BF16)
    kt_ref[0] = pt[:WIDTH]
    vt_ref[0] = pt[WIDTH:2 * WIDTH]
    ones_row = (lax.broadcasted_iota(jnp.int32, (V_ROWS - HEAD_DIM, ATT_TILE), 0) == 0).astype(BF16)
    for h in range(N_HEADS):
        rows = slice(WIDTH + h * HEAD_DIM, WIDTH + (h + 1) * HEAD_DIM)
        for c in range(tm // ATT_TILE):
            vta_ref[0, h, c, 0:HEAD_DIM, :] = pt[rows, c * ATT_TILE:(c + 1) * ATT_TILE].astype(BF16)
            vta_ref[0, h, c, HEAD_DIM:V_ROWS, :] = ones_row


def _proj_b_prompt_kernel(x_ref, gpre_ref, wn_ref, wt_ref, bf_ref, bfc_ref, tril_ref, place_ref, ones_ref,
                          qa_ref, ka_ref, vta_ref, kt_ref, vt_ref, z_ref, lft_ref, carry_ref,
                          *, tiles_per_seq):
    xn = _rmsnorm(x_ref[...], gpre_ref[...]).astype(BF16)
    p = _dot(xn, wn_ref[...])
    pt = _dot_nt(wt_ref[...], xn)
    wide = N_HEADS * LANES
    z_ref[...] = jax.nn.silu(p[:, 2 * wide:2 * wide + WIDTH]).astype(BF16)
    lf = _log_sigmoid(p[:, 2 * wide + WIDTH:] + bf_ref[...])
    lft_ref[0] = _log_sigmoid(pt[2 * WIDTH:2 * WIDTH + N_HEADS] + bfc_ref[...])

    @pl.when(pl.program_id(0) % tiles_per_seq == 0)
    def _():
        carry_ref[...] = jnp.zeros_like(carry_ref)
    carry = carry_ref[...]
    sums = []
    for c in range(x_ref.shape[0] // CHUNK):
        cum = _dot(tril_ref[...], lf[c * CHUNK:(c + 1) * CHUNK], precision=lax.Precision.HIGHEST) + carry
        sums.append(cum)
        carry = cum[CHUNK - 1:CHUNK, :]
    carry_ref[...] = carry
    cum = jnp.concatenate(sums, axis=0)
    hi = cum.astype(BF16)
    rest = cum - hi.astype(F32)
    mid = rest.astype(BF16)
    lo = (rest - mid.astype(F32)).astype(BF16)
    aug = _dot(jnp.concatenate([hi, mid, lo], axis=1), place_ref[...]) + ones_ref[...]
    qa = p[:, :wide] * ATTN_SCALE + aug[:, :wide]
    ka = p[:, wide:2 * wide] + aug[:, wide:]
    _store_attention_operands(qa, ka, pt, qa_ref, ka_ref, vta_ref, kt_ref, vt_ref)


def _decay_placement():
    wide = N_HEADS * LANES
    place = np.zeros((3 * LANES, 2 * wide), np.float32)
    ones = np.zeros((1, 2 * wide), np.float32)
    for h in range(N_HEADS):
        for piece in range(3):
            place[piece * LANES + h, h * LANES + HEAD_DIM + 3 + piece] = 1.0
            place[piece * LANES + h, wide + h * LANES + HEAD_DIM + piece] = -1.0
            ones[0, h * LANES + HEAD_DIM + piece] = 1.0
            ones[0, wide + h * LANES + HEAD_DIM + 3 + piece] = 1.0
    return jnp.asarray(place, BF16), jnp.asarray(ones, F32)


def _attention_operand_specs(b, t, tm):
    tps = t // tm
    per = tm // ATT_TILE
    shapes = [jax.ShapeDtypeStruct((b, N_HEADS, t, LANES), BF16), jax.ShapeDtypeStruct((b, N_HEADS, t, LANES), BF16),
              jax.ShapeDtypeStruct((b, N_HEADS, t // ATT_TILE, V_ROWS, ATT_TILE), BF16),
              jax.ShapeDtypeStruct((b, WIDTH, t), F32), jax.ShapeDtypeStruct((b, WIDTH, t), F32),
              jax.ShapeDtypeStruct((b * t, WIDTH), BF16)]
    heads = pl.BlockSpec((1, N_HEADS, tm, LANES), lambda i: (i // tps, 0, i % tps, 0))
    specs = [heads, heads,
             pl.BlockSpec((1, N_HEADS, per, V_ROWS, ATT_TILE), lambda i: (i // tps, 0, i % tps, 0, 0)),
             pl.BlockSpec((1, WIDTH, tm), lambda i: (i // tps, 0, i % tps)),
             pl.BlockSpec((1, WIDTH, tm), lambda i: (i // tps, 0, i % tps)),
             pl.BlockSpec((tm, WIDTH), lambda i: (i, 0))]
    return shapes, specs


def _proj_b_prompt(x, gpre, wn, wt, bf, bf_col, b, t, *, tm):
    m = x.shape[0]
    tps = t // tm
    shapes, specs = _attention_operand_specs(b, t, tm)
    place, ones = _decay_placement()
    tril = jnp.tril(jnp.ones((CHUNK, CHUNK), F32))
    return pl.pallas_call(
        functools.partial(_proj_b_prompt_kernel, tiles_per_seq=tps),
        grid=(m // tm,),
        in_specs=[pl.BlockSpec((tm, D_MODEL), lambda i: (i, 0)), _const_spec((1, D_MODEL)),
                  _const_spec(wn.shape), _const_spec(wt.shape), _const_spec((1, LANES)),
                  _const_spec((N_HEADS, 1)), _const_spec((CHUNK, CHUNK)),
                  _const_spec(place.shape), _const_spec(ones.shape)],
        out_specs=specs + [pl.BlockSpec((1, N_HEADS, tm), lambda i: (i // tps, 0, i % tps))],
        out_shape=shapes + [jax.ShapeDtypeStruct((b, N_HEADS, t), F32)],
        scratch_shapes=[pltpu.VMEM((1, LANES), F32)],
        compiler_params=_params("arbitrary"), name="proj_b_prompt",
    )(x, gpre, wn, wt, bf, bf_col, tril, place, ones)


def _proj_c_prompt_kernel(x_ref, gpre_ref, wn_ref, wt_ref, qa_ref, ka_ref, vta_ref, kt_ref, vt_ref, z_ref, km_ref):
    xn = _rmsnorm(x_ref[...], gpre_ref[...]).astype(BF16)
    p = _dot(xn, wn_ref[...])
    pt = _dot_nt(wt_ref[...], xn)
    wide = N_HEADS * LANES
    z_ref[...] = jax.nn.silu(p[:, 2 * wide:]).astype(BF16)
    ka = p[:, wide:2 * wide]
    for c in range(x_ref.shape[0] // MOBA_BLOCK):
        km_ref[0, c:c + 1, :] = jnp.mean(ka[c * MOBA_BLOCK:(c + 1) * MOBA_BLOCK], axis=0, keepdims=True)
    _store_attention_operands(p[:, :wide] * ATTN_SCALE, ka, pt, qa_ref, ka_ref, vta_ref, kt_ref, vt_ref)


def _proj_c_prompt(x, gpre, wn, wt, b, t, *, tm):
    m = x.shape[0]
    shapes, specs = _attention_operand_specs(b, t, tm)
    per = tm // MOBA_BLOCK
    return pl.pallas_call(
        _proj_c_prompt_kernel,
        grid=(m // tm,),
        in_specs=[pl.BlockSpec((tm, D_MODEL), lambda i: (i, 0)), _const_spec((1, D_MODEL)),
                  _const_spec(wn.shape), _const_spec(wt.shape)],
        out_specs=specs + [pl.BlockSpec((1, per, N_HEADS * LANES), lambda i: (i, 0, 0))],
        out_shape=shapes + [jax.ShapeDtypeStruct((m // tm, per, N_HEADS * LANES), F32)],
        compiler_params=_params("arbitrary"), name="proj_c_prompt",
    )(x, gpre, wn, wt)


def _out_kernel(x_ref, ya_ref, yb_ref, yc_ref, gpre_ref, wg_ref, woa_ref, wob_ref, woc_ref, wo_ref,
                gpost_ref, o_ref):
    x = x_ref[...]
    xn = _rmsnorm(x, gpre_ref[...]).astype(BF16)
    gates = jax.nn.sigmoid(_dot(xn, wg_ref[...]))
    merged = (gates[:, :D_MODEL] * _dot(ya_ref[...], woa_ref[...])
              + gates[:, D_MODEL:2 * D_MODEL] * _dot(yb_ref[...], wob_ref[...])
              + gates[:, 2 * D_MODEL:] * _dot(yc_ref[...], woc_ref[...]))
    r = _dot(merged.astype(BF16), wo_ref[...])
    o_ref[...] = x + _rmsnorm(r, gpost_ref[...])


def _out_proj(x, ya, yb, yc, gpre, wg, woa, wob, woc, wo, gpost, *, tm):
    m = x.shape[0]
    row = lambda i: (i, 0)
    return pl.pallas_call(
        _out_kernel,
        grid=(m // tm,),
        in_specs=[pl.BlockSpec((tm, D_MODEL), row)] + [pl.BlockSpec((tm, WIDTH), row)] * 3
                 + [_const_spec((1, D_MODEL)), _const_spec((D_MODEL, 3 * D_MODEL))]
                 + [_const_spec((WIDTH, D_MODEL))] * 3
                 + [_const_spec((D_MODEL, D_MODEL)), _const_spec((1, D_MODEL))],
        out_specs=pl.BlockSpec((tm, D_MODEL), row),
        out_shape=jax.ShapeDtypeStruct((m, D_MODEL), F32),
        compiler_params=_params("arbitrary"), name="out_proj",
    )(x, ya, yb, yc, gpre, wg, woa, wob, woc, wo, gpost)


def _softmax_step(s, vta, h, m_ref, acc_ref):
    m_prev = m_ref[h]
    m_new = jnp.maximum(m_prev, jnp.max(s, axis=0, keepdims=True))
    p = jnp.exp(s - m_new).astype(BF16)
    acc_ref[h] = jnp.exp(m_prev - m_new) * acc_ref[h] + _dot(vta, p)
    m_ref[h] = m_new


def _finish_attention(z_ref, y_ref, acc_ref):
    outs = [acc_ref[h, 0:HEAD_DIM, :] / acc_ref[h, HEAD_DIM:HEAD_DIM + 1, :] for h in range(N_HEADS)]
    o = jnp.concatenate(outs, axis=0).T
    y_ref[0] = (o * z_ref[0].astype(F32)).astype(BF16)


def _init_softmax(m_ref, acc_ref):
    m_ref[...] = jnp.full(m_ref.shape, NEG, F32)
    acc_ref[...] = jnp.zeros(acc_ref.shape, F32)


def _key_tile(ka_ref, h, j):
    start = j * ATT_TILE if isinstance(j, int) else pl.multiple_of(j * ATT_TILE, ATT_TILE)
    return ka_ref[0, h, pl.ds(start, ATT_TILE), :]


def _causal_tile():
    t = ATT_TILE
    return lax.broadcasted_iota(jnp.int32, (t, t), 0) <= lax.broadcasted_iota(jnp.int32, (t, t), 1)


def _tile_step(qa_ref, ka_ref, vta_ref, m_ref, acc_ref, j, bias=None, mask=None):
    def scores(h):
        s = _dot_nt(_key_tile(ka_ref, h, j), qa_ref[0, h])
        return s if bias is None else s + bias(h)

    pending = [scores(h) for h in range(SCORE_LEAD)]
    for h in range(N_HEADS):
        if h + SCORE_LEAD < N_HEADS:
            pending.append(scores(h + SCORE_LEAD))
        s = pending[h] if mask is None else jnp.where(mask, pending[h], NEG)
        _softmax_step(s, vta_ref[0, h, j], h, m_ref, acc_ref)


def _fox_kernel(qa_ref, ka_ref, vta_ref, z_ref, y_ref, m_ref, acc_ref):
    qi = pl.program_id(1)
    _init_softmax(m_ref, acc_ref)
    causal = _causal_tile()
    tile_step = functools.partial(_tile_step, qa_ref, ka_ref, vta_ref, m_ref, acc_ref)

    def body(j, carry):
        tile_step(j)
        return carry

    lax.fori_loop(0, qi, body, 0)
    tile_step(qi, mask=causal)
    _finish_attention(z_ref, y_ref, acc_ref)


def _attention_in_specs(t):
    n = t // ATT_TILE
    return [pl.BlockSpec((1, N_HEADS, ATT_TILE, LANES), lambda bi, qi: (bi, 0, qi, 0)),
            pl.BlockSpec((1, N_HEADS, t, LANES), lambda bi, qi: (bi, 0, 0, 0)),
            pl.BlockSpec((1, N_HEADS, n, V_ROWS, ATT_TILE), lambda bi, qi: (bi, 0, 0, 0, 0))]


def _attention_scratch():
    return [pltpu.VMEM((N_HEADS, 1, ATT_TILE), F32), pltpu.VMEM((N_HEADS, V_ROWS, ATT_TILE), F32)]


def _fox_attention(qa, ka, vta, z):
    b, _, t, _ = qa.shape
    tile = lambda bi, qi: (bi, qi, 0)
    return pl.pallas_call(
        _fox_kernel,
        grid=(b, t // ATT_TILE),
        in_specs=_attention_in_specs(t) + [pl.BlockSpec((1, ATT_TILE, WIDTH), tile)],
        out_specs=pl.BlockSpec((1, ATT_TILE, WIDTH), tile),
        out_shape=jax.ShapeDtypeStruct((b, t, WIDTH), BF16),
        scratch_shapes=_attention_scratch(),
        compiler_params=_params("arbitrary", "arbitrary"), name="fox_attention",
    )(qa, ka, vta, z)


def _moba_kernel(qa_ref, ka_ref, vta_ref, km_ref, tb_ref, far_ref, z_ref, y_ref, m_ref, acc_ref, pen_ref):
    qi = pl.program_id(1)
    t = ATT_TILE
    _init_softmax(m_ref, acc_ref)

    blk = lax.broadcasted_iota(jnp.int32, (MAX_BLOCKS, t), 0).astype(F32)
    own = qi.astype(F32)
    for h in range(N_HEADS):
        g = _dot_nt(km_ref[0, h].astype(BF16), qa_ref[0, h])
        g = jnp.where(blk < own, g, -jnp.inf)
        sel = jnp.zeros((MAX_BLOCKS, t), F32)
        for _ in range(MOBA_TOPK):
            best = jnp.max(g, axis=0, keepdims=True)
            cand = jnp.logical_and(g == best, best > -jnp.inf)
            first = jnp.min(jnp.where(cand, blk, float(MAX_BLOCKS)), axis=0, keepdims=True)
            pick = blk == first
            sel = jnp.where(pick, 1.0, sel)
            g = jnp.where(pick, -jnp.inf, g)
        bias = jnp.where(blk == own - 1.0, 0.0, far_ref[h:h + 1, 0:1])
        pen_ref[h] = jnp.where(sel > 0.5, bias, NEG)

    causal = _causal_tile()
    tile_step = functools.partial(_tile_step, qa_ref, ka_ref, vta_ref, m_ref, acc_ref)

    def body(j, carry):
        tile_step(j, bias=lambda h: pen_ref[h, pl.ds(j, 1), :])
        return carry

    lax.fori_loop(0, jnp.maximum(qi - 1, 0), body, 0)

    @pl.when(qi >= 1)
    def _():
        tile_step(qi - 1, bias=lambda h: pen_ref[h, pl.ds(qi - 1, 1), :] + tb_ref[h, 1])

    tile_step(qi, bias=lambda h: tb_ref[h, 0], mask=causal)
    _finish_attention(z_ref, y_ref, acc_ref)


def _moba_attention(qa, ka, vta, k_means, tb, far, z):
    b, _, t, _ = qa.shape
    tile = lambda bi, qi: (bi, qi, 0)
    return pl.pallas_call(
        _moba_kernel,
        grid=(b, t // ATT_TILE),
        in_specs=_attention_in_specs(t)
                 + [pl.BlockSpec((1, N_HEADS, MAX_BLOCKS, LANES), lambda bi, qi: (bi, 0, 0, 0)),
                    _const_spec((N_HEADS, 2, ATT_TILE, ATT_TILE)), _const_spec((N_HEADS, LANES)),
                    pl.BlockSpec((1, ATT_TILE, WIDTH), tile)],
        out_specs=pl.BlockSpec((1, ATT_TILE, WIDTH), tile),
        out_shape=jax.ShapeDtypeStruct((b, t, WIDTH), BF16),
        scratch_shapes=_attention_scratch() + [pltpu.VMEM((N_HEADS, MAX_BLOCKS, ATT_TILE), F32)],
        compiler_params=_params("arbitrary", "arbitrary"), name="moba_attention",
    )(qa, ka, vta, k_means, tb, far, z)


def _local_softmax(s):
    m = jnp.max(s, axis=-1, keepdims=True)
    p = jnp.exp(s - m)
    return m, jnp.sum(p, axis=-1, keepdims=True), p.astype(BF16)


def _combine(m_all, l_all, acc_ref, valid, new, n_pages):
    m_new, l_new, acc_new = new
    top = jnp.maximum(jnp.max(jnp.where(valid, m_all, NEG), axis=1, keepdims=True), m_new)
    w = jnp.where(valid, jnp.exp(m_all - top), 0.0)
    w_new = jnp.exp(m_new - top)
    total = jnp.sum(w * l_all, axis=1, keepdims=True) + w_new * l_new
    out = w_new * acc_new
    for p in range(n_pages):
        out = out + w[:, p:p + 1] * acc_ref[p]
    return out / total


def _sample_attn_kernel(pt_ref, *refs, n_pages, dec_seq, group):
    del pt_ref
    page_refs = [refs[5 * g:5 * g + 5] for g in range(group)]
    (qf_ref, qm_ref, fkn_ref, fvn_ref, mkn_ref, mvn_ref, cnc_ref, cnr_ref, sb_ref, nb_ref, fz_ref, mz_ref,
     yb_ref, yc_ref,
     mf_ref, lf_ref, accf_ref, mm_ref, lm_ref, accm_ref, gate_ref, carry_ref) = refs[5 * group:]
    step = pl.program_id(1)
    n_steps = n_pages // group
    rows = dec_seq * N_HEADS
    pages_per_block = MOBA_BLOCK // PAGE_SIZE
    lane = lax.broadcasted_iota(jnp.int32, (rows, LANES), 1)
    pages = [n_pages - 1 - (step * group + g) for g in range(group)]

    @pl.when(step == 0)
    def _():
        for ref in (mf_ref, lf_ref, mm_ref, lm_ref, gate_ref, carry_ref):
            ref[...] = jnp.zeros(ref.shape, F32)

    qf = qf_ref[0]
    qm = qm_ref[0]
    later = (lax.broadcasted_iota(jnp.int32, (PAGE_SIZE, PAGE_SIZE), 0)
             > lax.broadcasted_iota(jnp.int32, (PAGE_SIZE, PAGE_SIZE), 1)).astype(F32)
    fox_raw = [_dot(qf, page_refs[g][0][0, 0].astype(BF16)) for g in range(group)]
    moba_raw = [_dot(qm, page_refs[g][3][0, 0].astype(BF16)) for g in range(group)]
    in_page = [_dot(page_refs[g][2][0, 0], later, precision=lax.Precision.HIGHEST) for g in range(group)]

    carry = carry_ref[...]
    mf, lf = mf_ref[...], lf_ref[...]
    mm, lm, gate = mm_ref[...], lm_ref[...], gate_ref[...]
    fox_p, moba_p = [], []
    for g in range(group):
        suffix = in_page[g] + carry
        carry = carry + jnp.sum(page_refs[g][2][0, 0], axis=1, keepdims=True)
        m, l, p = _local_softmax(fox_raw[g] + jnp.concatenate([suffix] * dec_seq, axis=0) + cnc_ref[0])
        mf = jnp.where(lane == pages[g], m, mf)
        lf = jnp.where(lane == pages[g], l, lf)
        fox_p.append(p)
        local = group - 1 - g
        m, l, p = _local_softmax(moba_raw[g] + sb_ref[:, local * PAGE_SIZE:(local + 1) * PAGE_SIZE])
        mm = jnp.where(lane == pages[g], m, mm)
        lm = jnp.where(lane == pages[g], l, lm)
        moba_p.append(p)
    carry_ref[...] = carry
    mf_ref[...] = mf
    lf_ref[...] = lf

    for g in range(group):
        accf_ref[pages[g]] = _dot_nt(fox_p[g], page_refs[g][1][0, 0].astype(BF16))
        accm_ref[pages[g]] = _dot_nt(moba_p[g], page_refs[g][4][0, 0].astype(BF16))
    for g in range(0, group, pages_per_block):
        total = sum(jnp.sum(moba_raw[g + k], axis=1, keepdims=True) for k in range(pages_per_block))
        gate = jnp.where(lane == pages[g] // pages_per_block, total / MOBA_BLOCK, gate)
    mm_ref[...] = mm
    lm_ref[...] = lm
    gate_ref[...] = gate

    @pl.when(step == n_steps - 1)
    def _():
        n_blocks = n_pages // pages_per_block
        token = lax.broadcasted_iota(jnp.int32, (rows, LANES), 0) // N_HEADS
        causal = lane <= token
        head_cols = (lax.broadcasted_iota(jnp.int32, (N_HEADS, WIDTH), 1) // HEAD_DIM
                     == lax.broadcasted_iota(jnp.int32, (N_HEADS, WIDTH), 0)).astype(F32)

        def heads_to_tokens(o):
            return jnp.sum(o.reshape(dec_seq, N_HEADS, WIDTH) * head_cols[None], axis=1)

        def new_tokens(q, k_ref, v_ref, extra):
            s = jnp.where(causal, _dot_nt(q, k_ref[0]) + extra, NEG)
            m, l, p = _local_softmax(s)
            return m, l, _dot(p, v_ref[0])

        new = new_tokens(qf, fkn_ref, fvn_ref, cnc_ref[0] - cnr_ref[0])
        out = _combine(mf_ref[...], lf_ref[...], accf_ref, lane < n_pages, new, n_pages)
        yb_ref[0] = heads_to_tokens(out) * fz_ref[0].astype(F32)

        lane_f = lane.astype(F32)
        g = jnp.where(lane < n_blocks, gate_ref[...], -jnp.inf)
        sel = jnp.zeros((rows, LANES), F32)
        for _ in range(min(MOBA_TOPK, n_blocks)):
            best = jnp.max(g, axis=1, keepdims=True)
            first = jnp.min(jnp.where(g == best, lane_f, float(LANES)), axis=1, keepdims=True)
            pick = lane_f == first
            sel = jnp.where(pick, 1.0, sel)
            g = jnp.where(pick, -jnp.inf, g)
        to_pages = (lax.broadcasted_iota(jnp.int32, (LANES, LANES), 1) // pages_per_block
                    == lax.broadcasted_iota(jnp.int32, (LANES, LANES), 0)).astype(BF16)
        sel_pages = _dot(sel.astype(BF16), to_pages) > 0.5
        new = new_tokens(qm, mkn_ref, mvn_ref, nb_ref[...])
        valid = jnp.logical_and(lane < n_pages, sel_pages)
        out = _combine(mm_ref[...], lm_ref[...], accm_ref, valid, new, n_pages)
        yc_ref[0] = heads_to_tokens(out) * mz_ref[0].astype(F32)


def _sample_attention(layer, page_table, caches, qf, qm, fkn, fvn, mkn, mvn, cnc, cnr, sb, nb, fz, mz):
    cache_fk, cache_fv, cache_fl_t, cache_mk, cache_mv = caches
    nb_seq, n_pages = page_table.shape
    rows = qf.shape[1]
    dec_seq = rows // N_HEADS
    pt = page_table.reshape(-1)

    group = PAGES_PER_STEP
    n_steps = n_pages // group
    assert n_pages % group == 0 and group % (MOBA_BLOCK // PAGE_SIZE) == 0

    def page(g):
        return lambda bi, si, pt_ref: (layer, pt_ref[bi * n_pages + n_pages - 1 - (si * group + g)], 0, 0)

    per_seq = lambda bi, si, pt_ref: (bi, 0, 0)
    new_kv = pl.BlockSpec((1, LANES, WIDTH), per_seq)
    out_spec = pl.BlockSpec((1, dec_seq, WIDTH), per_seq)
    stats = pltpu.VMEM((rows, LANES), F32)
    page_specs, page_args = [], []
    for g in range(group):
        kv_page = pl.BlockSpec((1, 1, WIDTH, PAGE_SIZE), page(g))
        page_specs += [kv_page, kv_page, pl.BlockSpec((1, 1, N_HEADS, PAGE_SIZE), page(g)), kv_page, kv_page]
        page_args += [cache_fk, cache_fv, cache_fl_t, cache_mk, cache_mv]
    grid_spec = pltpu.PrefetchScalarGridSpec(
        num_scalar_prefetch=1,
        grid=(nb_seq, n_steps),
        in_specs=page_specs
                 + [pl.BlockSpec((1, rows, WIDTH), per_seq), pl.BlockSpec((1, rows, WIDTH), per_seq),
                    new_kv, new_kv, new_kv, new_kv,
                    pl.BlockSpec((1, rows, 1), per_seq), pl.BlockSpec((1, rows, LANES), per_seq),
                    pl.BlockSpec((rows, group * PAGE_SIZE), lambda bi, si, pt_ref: (0, n_steps - 1 - si)),
                    pl.BlockSpec((rows, LANES), lambda bi, si, pt_ref: (0, 0)),
                    out_spec, out_spec],
        out_specs=[out_spec, out_spec],
        scratch_shapes=[stats, stats, pltpu.VMEM((n_pages, rows, WIDTH), F32),
                        stats, stats, pltpu.VMEM((n_pages, rows, WIDTH), F32),
                        stats, pltpu.VMEM((N_HEADS, 1), F32)],
    )
    return pl.pallas_call(
        functools.partial(_sample_attn_kernel, n_pages=n_pages, dec_seq=dec_seq, group=group),
        grid_spec=grid_spec,
        out_shape=[jax.ShapeDtypeStruct((nb_seq, dec_seq, WIDTH), F32)] * 2,
        compiler_params=_params("arbitrary", "arbitrary"), name="sample_attention",
    )(pt, *page_args, qf, qm, fkn, fvn, mkn, mvn, cnc, cnr, sb, nb, fz, mz)


def _spread_heads(w):
    d = w.shape[0]
    return jnp.pad(w.reshape(d, N_HEADS, HEAD_DIM), ((0, 0), (0, 0), (0, LANES - HEAD_DIM))).reshape(d, N_HEADS * LANES)


def _layer_weights(w_in, g_pre, g_post, g_sgu, w_s, b_s, b_f, w_oa, w_ob, w_oc, w_o):
    cols, off = [], 0
    for size in (3 * WIDTH, WIDTH, WIDTH, WIDTH, N_HEADS, WIDTH, WIDTH, WIDTH, WIDTH, WIDTH, 3 * D_MODEL):
        cols.append(w_in[:, off:off + size])
        off += size
    wa, bq, bk, bv, bfl, bz, cq, ck, cv, cz, wg = cols
    bfl_pad = jnp.pad(bfl, ((0, 0), (0, LANES - N_HEADS)))
    bf16 = lambda a: a.astype(BF16)
    return dict(
        wa=bf16(wa), wg=bf16(wg),
        wb=bf16(jnp.concatenate([bq, bk, bv, bz, bfl_pad], axis=1)),
        wc=bf16(jnp.concatenate([cq, ck, cv, cz], axis=1)),
        wb_n=bf16(jnp.concatenate([_spread_heads(bq), _spread_heads(bk), bz, bfl_pad], axis=1)),
        wb_t=bf16(jnp.concatenate([bk, bv, bfl_pad[:, :BF16_SUBLANES]], axis=1).T),
        wc_n=bf16(jnp.concatenate([_spread_heads(cq), _spread_heads(ck), cz], axis=1)),
        wc_t=bf16(jnp.concatenate([ck, cv], axis=1).T),
        g_pre=g_pre[None, :], g_post=g_post[None, :], g_sgu=g_sgu[None, :],
        w_s=w_s, b_s=b_s, b_f=jnp.pad(b_f, (0, LANES - N_HEADS))[None, :], b_f_col=b_f[:, None],
        w_oa=bf16(w_oa), w_ob=bf16(w_ob), w_oc=bf16(w_oc), w_o=bf16(w_o))


def _prompt_layer(x, w, tb, far):
    b, t, _ = x.shape
    tm = min(ROW_TILE, t)
    x2 = x.reshape(b * t, D_MODEL)
    ya = _proj_a(x2, w["g_pre"], w["wa"], w["g_sgu"], w["w_s"], w["b_s"].T, tm=tm)
    qb, kb, vb, kbt, vbt, zb, lft = _proj_b_prompt(x2, w["g_pre"], w["wb_n"], w["wb_t"], w["b_f"], w["b_f_col"],
                                                   b, t, tm=tm)
    qc, kc, vc, kct, vct, zc, km = _proj_c_prompt(x2, w["g_pre"], w["wc_n"], w["wc_t"], b, t, tm=tm)
    seq = lambda a: a.reshape(b, t, a.shape[-1])
    yb = _fox_attention(qb, kb, vb, seq(zb))
    n_blocks = t // MOBA_BLOCK
    km = km.reshape(b, n_blocks, N_HEADS, LANES).transpose(0, 2, 1, 3)
    km = jnp.pad(km, ((0, 0), (0, 0), (0, MAX_BLOCKS - n_blocks), (0, 0)))
    yc = _moba_attention(qc, kc, vc, km, tb, far, seq(zc))
    out = _out_proj(x2, ya, yb.reshape(b * t, WIDTH), yc.reshape(b * t, WIDTH), w["g_pre"], w["wg"],
                    w["w_oa"], w["w_ob"], w["w_oc"], w["w_o"], w["g_post"], tm=tm)
    heads = lambda a: a.reshape(b, N_HEADS, HEAD_DIM, t).transpose(0, 3, 1, 2)
    return out.reshape(b, t, D_MODEL), (heads(kbt), heads(vbt), lft.transpose(0, 2, 1), heads(kct), heads(vct))


def _sample_layer(x, layer, page_table, caches, w, sb, nb):
    s, d, _ = x.shape
    m = s * d
    rows = d * N_HEADS
    x2 = x.reshape(m, D_MODEL)
    ya, vn = _proj_a_sample(x2, w["g_pre"], w["wa"], w["g_sgu"], w["w_s"], w["b_s"], dec_seq=d)
    seg = jnp.asarray(np.kron(np.eye(s, dtype=np.float32), np.tril(np.ones((d, d), np.float32))))
    qb, kb, vb, kb16, vb16, zb, lf, cf = _proj_b(x2, w["g_pre"], w["wb"], w["b_f"], seg)
    qc, kc, vc, kc16, vc16, zc = _proj_c(x2, w["g_pre"], w["wc"])

    head_cols = (jnp.arange(WIDTH)[None, :] // HEAD_DIM == jnp.arange(N_HEADS)[:, None]).astype(BF16)
    per_head = lambda q: (q.reshape(s, d, 1, WIDTH) * head_cols[None, None]).reshape(s, rows, WIDTH)
    new_rows = lambda a: jnp.pad(a.reshape(s, d, WIDTH), ((0, 0), (0, LANES - d), (0, 0)))
    cum = cf.reshape(s, d, N_HEADS)
    cnc = cum.reshape(s, rows, 1)
    cnr = jnp.pad(jnp.broadcast_to(cum.transpose(0, 2, 1)[:, None], (s, d, N_HEADS, d)).reshape(s, rows, d),
                  ((0, 0), (0, 0), (0, LANES - d)))
    seq = lambda a: a.reshape(s, d, WIDTH)
    yb, yc = _sample_attention(layer, page_table, caches, per_head(qb), per_head(qc),
                               new_rows(kb16), new_rows(vb16), new_rows(kc16), new_rows(vc16),
                               cnc, cnr, sb, nb, seq(zb), seq(zc))
    out = _out_proj(x2, ya, yb.reshape(m, WIDTH).astype(BF16), yc.reshape(m, WIDTH).astype(BF16),
                    w["g_pre"], w["wg"], w["w_oa"], w["w_ob"], w["w_oc"], w["w_o"], w["g_post"], tm=m)
    return out.reshape(s, d, D_MODEL), (kb, vb, lf, kc, vc, vn)


def _cache_views(cache_fox_k, cache_fox_v, cache_fox_logf, cache_moba_k, cache_moba_v):
    depth, n_pool = cache_fox_k.shape[:2]
    kv = lambda c: c.transpose(0, 1, 3, 4, 2).reshape(depth, n_pool, WIDTH, PAGE_SIZE)
    return (kv(cache_fox_k), kv(cache_fox_v), cache_fox_logf.transpose(0, 1, 3, 2),
            kv(cache_moba_k), kv(cache_moba_v))


def kernel(x_prompt, x_sample, cache_fox_k, cache_fox_v, cache_fox_logf, cache_moba_k, cache_moba_v,
           page_table, w_in, g_pre, g_post, g_sgu, w_spatial, b_spatial, b_forget,
           w_out_a, w_out_b, w_out_c, w_out, rel_bias):
    depth = cache_fox_k.shape[0]
    past_len = page_table.shape[1] * PAGE_SIZE
    dec_seq = x_sample.shape[1]
    assert x_prompt.shape[1] % ROW_TILE == 0 and x_prompt.shape[1] // MOBA_BLOCK <= MAX_BLOCKS
    assert past_len % MOBA_BLOCK == 0 and dec_seq <= MOBA_BLOCK and past_len // MOBA_BLOCK <= LANES

    tb, far, sb, nb = _bias_tables(rel_bias, past_len, dec_seq)
    caches = _cache_views(cache_fox_k, cache_fox_v, cache_fox_logf, cache_moba_k, cache_moba_v)

    hp, hs = x_prompt, x_sample
    p_rows, s_rows = [], []
    for l in range(depth):
        w = _layer_weights(w_in[l], g_pre[l], g_post[l], g_sgu[l], w_spatial[l], b_spatial[l], b_forget[l],
                           w_out_a[l], w_out_b[l], w_out_c[l], w_out[l])
        hp, rp = _prompt_layer(hp, w, tb, far)
        hs, rs = _sample_layer(hs, l, page_table, caches, w, sb, nb)
        p_rows.append(rp)
        s_rows.append(rs)
    stack = lambda rows, i: jnp.stack([r[i] for r in rows])
    s, d = x_sample.shape[:2]
    sample_shapes = [(N_HEADS, HEAD_DIM), (N_HEADS, HEAD_DIM), (N_HEADS,), (N_HEADS, HEAD_DIM), (N_HEADS, HEAD_DIM),
                     (A_GROUPS, A_GROUP_DIM)]
    sample_out = tuple(stack(s_rows, i).reshape((depth, s, d) + shape) for i, shape in enumerate(sample_shapes))
    return (hp, hs) + tuple(stack(p_rows, i) for i in range(5)) + sample_out
```
